```python
import jax, jax.numpy as jnp
from jax import lax
import numpy as np

D_MODEL = 2048
BATCH = 4
SEQ = 2048
DEPTH = 4
DEC_BATCH = 128
DEC_SEQ = 1
PAST_LEN = 16384
PAGE_SIZE = 128

GLA_HEADS = 4
GLA_KEY_W = D_MODEL // 2
GLA_VAL_W = D_MODEL
GLA_DK = GLA_KEY_W // GLA_HEADS
GLA_DV = GLA_VAL_W // GLA_HEADS
GATE_RANK = 16
GATE_TEMP = 16.0
CHUNK = 64
CONV_W = D_MODEL
CONV_K = 3
D_FF = 5632
N_EXPERTS = 8
TOP_K = 2
N_DENSE = (DEPTH + 1) // 2
N_MOE = DEPTH // 2
EPS = 1e-6

SPLITS = (GLA_KEY_W, GLA_KEY_W, GLA_VAL_W, GLA_VAL_W, GATE_RANK, CONV_W, CONV_W, CONV_W, D_MODEL, D_MODEL)
IN_W = sum(SPLITS)
SPLIT_IDX = tuple(sum(SPLITS[:i + 1]) for i in range(len(SPLITS) - 1))

kernel_name = "gla_shortconv_adaln_moe_decoder_step"


def rms_norm(x, gain):
    xf = x.astype(jnp.float32)
    y = xf * lax.rsqrt(jnp.mean(xf * xf, axis=-1, keepdims=True) + EPS) * gain.astype(jnp.float32)
    return y.astype(x.dtype)


def gla_recurrence(q, k, v, log_a, s0):
    b_, t_ = q.shape[0], q.shape[1]
    c_ = min(CHUNK, t_)
    n_ = -(-t_ // c_)
    pad = n_ * c_ - t_

    def blocks(a):
        a = jnp.pad(a.astype(jnp.float32), ((0, 0), (0, pad), (0, 0), (0, 0)))
        return a.reshape(b_, n_, c_, GLA_HEADS, a.shape[-1]).transpose(1, 0, 3, 2, 4)

    causal = jnp.tril(jnp.ones((c_, c_), dtype=bool))[:, :, None]

    def step(s, blk):
        qc, kc, vc, ac = blk
        cum = jnp.cumsum(ac, axis=2)
        diff = cum[:, :, :, None, :] - cum[:, :, None, :, :]
        decay = jnp.where(causal, jnp.exp(jnp.where(causal, diff, 0.0)), 0.0)
        scores = jnp.einsum('bhijd,bhjd->bhij', qc[:, :, :, None, :] * decay, kc)
        o = jnp.einsum('bhij,bhje->bhie', scores, vc) + jnp.einsum('bhid,bhde->bhie', qc * jnp.exp(cum), s)
        last = cum[:, :, -1:, :]
        s_new = jnp.exp(last[:, :, 0, :, None]) * s + jnp.einsum('bhjd,bhje->bhde', kc * jnp.exp(last - cum), vc)
        return s_new, o

    s_fin, o = lax.scan(step, s0.astype(jnp.float32), (blocks(q), blocks(k), blocks(v), blocks(log_a)))
    o = o.transpose(1, 0, 3, 2, 4).reshape(b_, n_ * c_, GLA_HEADS, GLA_DV)[:, :t_]
    return o, s_fin


def token_mixer(h, s_gla, s_conv, w_in, w_fg2, b_fg2, gla_gain, conv_w, w_pa, w_pb, w_o):
    b_, t_, _ = h.shape
    q, k, v, g, fg, cb, cc, ch, ga, gb = jnp.split(h @ w_in, SPLIT_IDX, axis=-1)
    log_a = jax.nn.log_sigmoid((fg @ w_fg2 + b_fg2).astype(jnp.float32)) / GATE_TEMP
    q = q.reshape(b_, t_, GLA_HEADS, GLA_DK) * (GLA_DK ** -0.5)
    k = k.reshape(b_, t_, GLA_HEADS, GLA_DK)
    v = v.reshape(b_, t_, GLA_HEADS, GLA_DV)
    log_a = log_a.reshape(b_, t_, GLA_HEADS, GLA_DK)
    o, s_gla_new = gla_recurrence(q, k, v, log_a, s_gla)
    o = o * lax.rsqrt(jnp.mean(o * o, axis=-1, keepdims=True) + EPS) * gla_gain.astype(jnp.float32)
    o = o.reshape(b_, t_, GLA_VAL_W).astype(h.dtype) * jax.nn.silu(g)
    y_a = o @ w_pa
    u = cc * ch
    u_ext = jnp.concatenate([s_conv.astype(u.dtype), u], axis=1)
    conv = conv_w[0] * u_ext[:, 0:t_]
    for i in range(1, CONV_K):
        conv = conv + conv_w[i] * u_ext[:, i:i + t_]
    y_b = (cb * conv) @ w_pb
    y = jax.nn.sigmoid(ga) * y_a + jax.nn.sigmoid(gb) * y_b
    return y @ w_o, s_gla_new, u_ext[:, -(CONV_K - 1):]


def swiglu(h, w1, w3, w2):
    return (jax.nn.silu(h @ w1) * (h @ w3)) @ w2


def moe_ffn(h, router, w1, w3, w2):
    shp = h.shape
    hf = h.reshape(-1, shp[-1])
    logits = (hf @ router).astype(jnp.float32)
    top_v, top_i = lax.top_k(logits, TOP_K)
    top_w = jax.nn.softmax(top_v, axis=-1)
    gates = jnp.einsum('nk,nke->ne', top_w, jax.nn.one_hot(top_i, N_EXPERTS, dtype=jnp.float32)).astype(h.dtype)
    out = jnp.zeros_like(hf)
    for e in range(N_EXPERTS):
        out = out + gates[:, e:e + 1] * swiglu(hf, w1[e], w3[e], w2[e])
    return out.reshape(shp)


def run_trunk(x, c, state_gla, state_conv, w_ada, b_ada, norm1, norm2, w_in, w_fg2, b_fg2, gla_gain,
              conv_w, w_pa, w_pb, w_o, dense_w1, dense_w3, dense_w2, router, moe_w1, moe_w3, moe_w2, final_norm):
    c_act = jax.nn.silu(c)
    new_gla, new_conv = [], []
    for l in range(DEPTH):
        mod = c_act @ w_ada[l] + b_ada[l]
        sh1, sc1, g1, sh2, sc2, g2 = [m[:, None, :] for m in jnp.split(mod, 6, axis=-1)]
        h = rms_norm(x, norm1[l]) * (1 + sc1) + sh1
        y, sg, sc = token_mixer(h, state_gla[l], state_conv[l], w_in[l], w_fg2[l], b_fg2[l], gla_gain[l],
                                conv_w[l], w_pa[l], w_pb[l], w_o[l])
        x = x + g1 * y
        h = rms_norm(x, norm2[l]) * (1 + sc2) + sh2
        if l % 2 == 0:
            f = swiglu(h, dense_w1[l // 2], dense_w3[l // 2], dense_w2[l // 2])
        else:
            f = moe_ffn(h, router[l // 2], moe_w1[l // 2], moe_w3[l // 2], moe_w2[l // 2])
        x = x + g2 * f
        new_gla.append(sg.astype(state_gla.dtype))
        new_conv.append(sc.astype(state_conv.dtype))
    return rms_norm(x, final_norm), jnp.stack(new_gla), jnp.stack(new_conv)


def setup_inputs(seed: int = 0) -> dict:
    key = jax.random.key(seed)
    ks = jax.random.split(key, 32)
    f32 = jnp.float32
    nrm = lambda k, shape, s: jax.random.normal(k, shape, f32) * s
    return {
        "x_prompt": nrm(ks[0], (BATCH, SEQ, D_MODEL), 1.0),
        "x_sample": nrm(ks[1], (DEC_BATCH, DEC_SEQ, D_MODEL), 1.0),
        "state_gla": nrm(ks[2], (DEPTH, DEC_BATCH, GLA_HEADS, GLA_DK, GLA_DV), 1.0),
        "state_conv": nrm(ks[3], (DEPTH, DEC_BATCH, CONV_K - 1, CONV_W), 1.0),
        "c_prompt": nrm(ks[4], (BATCH, D_MODEL), 1.0),
        "c_sample": nrm(ks[5], (DEC_BATCH, D_MODEL), 1.0),
        "w_ada": nrm(ks[6], (DEPTH, D_MODEL, 6 * D_MODEL), 0.5 * D_MODEL ** -0.5),
        "b_ada": nrm(ks[7], (DEPTH, 6 * D_MODEL), 0.02),
        "norm1": 1.0 + nrm(ks[8], (DEPTH, D_MODEL), 0.02),
        "norm2": 1.0 + nrm(ks[9], (DEPTH, D_MODEL), 0.02),
        "w_in": nrm(ks[10], (DEPTH, D_MODEL, IN_W), D_MODEL ** -0.5),
        "w_fg2": nrm(ks[11], (DEPTH, GATE_RANK, GLA_KEY_W), GATE_RANK ** -0.5),
        "b_fg2": nrm(ks[12], (DEPTH, GLA_KEY_W), 0.1),
        "gla_gain": 1.0 + nrm(ks[13], (DEPTH, GLA_DV), 0.02),
        "conv_w": nrm(ks[14], (DEPTH, CONV_K, CONV_W), CONV_K ** -0.5),
        "w_pa": nrm(ks[15], (DEPTH, GLA_VAL_W, D_MODEL), GLA_VAL_W ** -0.5),
        "w_pb": nrm(ks[16], (DEPTH, CONV_W, D_MODEL), CONV_W ** -0.5),
        "w_o": nrm(ks[17], (DEPTH, D_MODEL, D_MODEL), D_MODEL ** -0.5),
        "dense_w1": nrm(ks[18], (N_DENSE, D_MODEL, D_FF), D_MODEL ** -0.5),
        "dense_w3": nrm(ks[19], (N_DENSE, D_MODEL, D_FF), D_MODEL ** -0.5),
        "dense_w2": nrm(ks[20], (N_DENSE, D_FF, D_MODEL), D_FF ** -0.5),
        "router": nrm(ks[21], (N_MOE, D_MODEL, N_EXPERTS), D_MODEL ** -0.5),
        "moe_w1": nrm(ks[22], (N_MOE, N_EXPERTS, D_MODEL, D_FF), D_MODEL ** -0.5),
        "moe_w3": nrm(ks[23], (N_MOE, N_EXPERTS, D_MODEL, D_FF), D_MODEL ** -0.5),
        "moe_w2": nrm(ks[24], (N_MOE, N_EXPERTS, D_FF, D_MODEL), D_FF ** -0.5),
        "final_norm": 1.0 + nrm(ks[25], (D_MODEL,), 0.02),
    }


def reference(x_prompt, x_sample, state_gla, state_conv, c_prompt, c_sample, w_ada, b_ada, norm1, norm2,
              w_in, w_fg2, b_fg2, gla_gain, conv_w, w_pa, w_pb, w_o, dense_w1, dense_w3, dense_w2,
              router, moe_w1, moe_w3, moe_w2, final_norm):
    weights = (w_ada, b_ada, norm1, norm2, w_in, w_fg2, b_fg2, gla_gain, conv_w, w_pa, w_pb, w_o,
               dense_w1, dense_w3, dense_w2, router, moe_w1, moe_w3, moe_w2, final_norm)
    zero_gla = jnp.zeros((DEPTH, BATCH, GLA_HEADS, GLA_DK, GLA_DV), state_gla.dtype)
    zero_conv = jnp.zeros((DEPTH, BATCH, CONV_K - 1, CONV_W), state_conv.dtype)
    y_prompt, gla_prompt, conv_prompt = run_trunk(x_prompt, c_prompt, zero_gla, zero_conv, *weights)
    y_sample, gla_sample, conv_sample = run_trunk(x_sample, c_sample, state_gla, state_conv, *weights)
    return (y_prompt, y_sample, gla_prompt, conv_prompt, gla_sample, conv_sample)
```

```python
import functools

import numpy as np
import jax
import jax.numpy as jnp
from jax import lax
from jax.experimental import pallas as pl
from jax.experimental.pallas import tpu as pltpu

F32 = jnp.float32
BF16 = jnp.bfloat16
EPS = 1e-6
GATE_TEMP = 16.0
TOP_K = 2
LANES = 128
VMEM_LIMIT_BYTES = 56 * 1024 * 1024


def _params(*sem):
    return pltpu.CompilerParams(dimension_semantics=sem, vmem_limit_bytes=VMEM_LIMIT_BYTES)


def _dot(a, b):
    return jnp.dot(a, b, preferred_element_type=F32)


def _dot_nt(a, b):
    return lax.dot_general(a, b, (((1,), (1,)), ((), ())), preferred_element_type=F32)


def _split(x):
    hi = x.astype(BF16)
    lo = (x - hi.astype(F32)).astype(BF16)
    return hi, lo


def _dot_f32(a, b):
    ah, al = _split(a)
    bh, bl = _split(b)
    return _dot(ah, bh) + (_dot(ah, bl) + _dot(al, bh))


def _silu(x):
    return x * jax.nn.sigmoid(x)


def _log_sigmoid(x):
    return jnp.minimum(x, 0.0) - jnp.log1p(jnp.exp(-jnp.abs(x)))


def _norm_mod(x, gain, scale, shift):
    y = x * lax.rsqrt(jnp.mean(x * x, axis=-1, keepdims=True) + EPS) * gain
    return y * (1.0 + scale) + shift


def _pick(n, pref):
    t = min(n, pref)
    while n % t:
        t //= 2
    return t


def _adaln_kernel(c_ref, w_ref, b_ref, o_ref):
    a = _silu(c_ref[...]).astype(BF16)
    o_ref[...] = _dot(a, w_ref[...].astype(BF16)) + b_ref[...]


def _adaln(c, w_ada, b_ada):
    depth, d, n = w_ada.shape
    r = c.shape[0]
    tn = _pick(n, 1024)
    return pl.pallas_call(
        _adaln_kernel,
        grid=(depth, n // tn),
        in_specs=[
            pl.BlockSpec((r, d), lambda l, j: (0, 0)),
            pl.BlockSpec((None, d, tn), lambda l, j: (l, 0, j)),
            pl.BlockSpec((None, 1, tn), lambda l, j: (l, 0, j)),
        ],
        out_specs=pl.BlockSpec((None, r, tn), lambda l, j: (l, 0, j)),
        out_shape=jax.ShapeDtypeStruct((depth, r, n), F32),
        compiler_params=_params("arbitrary", "arbitrary"),
        name="adaln",
    )(c, w_ada, b_ada.reshape(depth, 1, n))


def _win_kernel(x_ref, sh_ref, sc_ref, gain_ref, w_ref, wfg_ref, proj_ref, fg_ref, h_scr):
    @pl.when(pl.program_id(1) == 0)
    def _():
        h = _norm_mod(x_ref[...], gain_ref[...], sc_ref[...], sh_ref[...]).astype(BF16)
        h_scr[...] = h
        fg_ref[...] = _dot(h, wfg_ref[...])

    proj_ref[...] = _dot(h_scr[...], w_ref[...]).astype(proj_ref.dtype)


def _win(x, mod, gain, w_main, w_fg, rows_per_group, tm, out_dtype):
    m, d = x.shape
    n = w_main.shape[1]
    r = mod.shape[1]
    tn = _pick(n, 1024)
    grp = lambda i: (i * tm) // rows_per_group
    return pl.pallas_call(
        _win_kernel,
        grid=(m // tm, n // tn),
        in_specs=[
            pl.BlockSpec((tm, d), lambda i, j: (i, 0)),
            pl.BlockSpec((None, r, d), lambda i, j: (grp(i), 0, 0)),
            pl.BlockSpec((None, r, d), lambda i, j: (grp(i), 0, 1)),
            pl.BlockSpec((1, d), lambda i, j: (0, 0)),
            pl.BlockSpec((d, tn), lambda i, j: (0, j)),
            pl.BlockSpec((d, LANES), lambda i, j: (0, 0)),
        ],
        out_specs=[
            pl.BlockSpec((tm, tn), lambda i, j: (i, j)),
            pl.BlockSpec((tm, LANES), lambda i, j: (i, 0)),
        ],
        out_shape=[jax.ShapeDtypeStruct((m, n), out_dtype), jax.ShapeDtypeStruct((m, LANES), F32)],
        scratch_shapes=[pltpu.VMEM((tm, d), BF16)],
        compiler_params=_params("arbitrary", "arbitrary"),
        name="win",
    )(x, mod, mod, gain, w_main, w_fg)


def _gla_tables(c):
    i = np.arange(c)[:, None]
    j = np.arange(c)[None, :]
    coefs = [(j <= i), (j > i)]
    masks = [(i == j)]
    b = c
    while b >= 2:
        mid = (i // b) * b + b // 2 - 1
        coefs.append(((j > mid) & (j <= i)) | ((j > i) & (j <= mid)))
        masks.append((i // b == j // b) & (i % b >= b // 2) & (j % b < b // 2))
        b //= 2
    return np.concatenate(coefs, 0).astype(np.float32), np.stack(masks).astype(np.float32)


def _gla_chunk_kernel(q_ref, k_ref, v_ref, g_ref, fg_ref, wfg2_ref, bfg_ref, gain_ref, coef_ref, mask_ref,
                      o_ref, s_out_ref, s_scr, *, scale, chunk, levels):
    ci = pl.program_id(2)

    @pl.when(ci == 0)
    def _():
        s_scr[...] = jnp.zeros_like(s_scr)

    c = chunk
    log_a = _log_sigmoid(_dot_f32(fg_ref[...], wfg2_ref[...]) + bfg_ref[...]) * (1.0 / GATE_TEMP)
    la_hi, la_lo = _split(log_a)
    coef = coef_ref[...]
    decay = jnp.exp(_dot(coef, la_hi) + _dot(coef, la_lo))

    q = q_ref[...].astype(F32) * scale
    k = k_ref[...].astype(F32)
    v = v_ref[...].astype(BF16)
    d_cum = decay[0:c]
    d_tail = decay[c:2 * c]

    scores = mask_ref[0] * _dot_nt(q.astype(BF16), k.astype(BF16))
    for lv in range(levels):
        d_lv = decay[(2 + lv) * c:(3 + lv) * c]
        scores = scores + mask_ref[1 + lv] * _dot_nt((q * d_lv).astype(BF16), (k * d_lv).astype(BF16))

    s = s_scr[...]
    o = _dot(scores.astype(BF16), v) + _dot((q * d_cum).astype(BF16), s.astype(BF16))

    nt = min(c, LANES)
    a_col = jnp.transpose(d_cum[c - nt:c])[:, nt - 1:nt]
    k_tail_t = jnp.transpose(k * d_tail).astype(BF16)
    s_new = a_col * s + _dot(k_tail_t, v)
    s_scr[...] = s_new

    o = o * lax.rsqrt(jnp.mean(o * o, axis=-1, keepdims=True) + EPS) * gain_ref[...]
    o_ref[...] = (o * _silu(g_ref[...].astype(F32))).astype(o_ref.dtype)

    @pl.when(ci == pl.num_programs(2) - 1)
    def _():
        s_out_ref[...] = s_new


def _gla_chunk(proj, fg, w_fg2, b_fg2, gain, batch, seq, heads, dk, dv, chunk):
    m = proj.shape[0]
    kw, vw = heads * dk, heads * dv
    nc = seq // chunk
    coef, mask = _gla_tables(chunk)
    levels = mask.shape[0] - 1
    rank_pad = w_fg2.shape[0]
    row = lambda b, h, c: b * nc + c
    kern = functools.partial(_gla_chunk_kernel, scale=float(dk) ** -0.5, chunk=chunk, levels=levels)
    return pl.pallas_call(
        kern,
        grid=(batch, heads, nc),
        in_specs=[
            pl.BlockSpec((chunk, dk), lambda b, h, c: (row(b, h, c), h)),
            pl.BlockSpec((chunk, dk), lambda b, h, c: (row(b, h, c), kw // dk + h)),
            pl.BlockSpec((chunk, dv), lambda b, h, c: (row(b, h, c), (2 * kw) // dv + h)),
            pl.BlockSpec((chunk, dv), lambda b, h, c: (row(b, h, c), (2 * kw + vw) // dv + h)),
            pl.BlockSpec((chunk, LANES), lambda b, h, c: (row(b, h, c), 0)),
            pl.BlockSpec((rank_pad, dk), lambda b, h, c: (0, h)),
            pl.BlockSpec((1, dk), lambda b, h, c: (0, h)),
            pl.BlockSpec((1, dv), lambda b, h, c: (0, 0)),
            pl.BlockSpec(coef.shape, lambda b, h, c: (0, 0)),
            pl.BlockSpec(mask.shape, lambda b, h, c: (0, 0, 0)),
        ],
        out_specs=[
            pl.BlockSpec((chunk, dv), lambda b, h, c: (row(b, h, c), h)),
            pl.BlockSpec((None, None, dk, dv), lambda b, h, c: (b, h, 0, 0)),
        ],
        out_shape=[jax.ShapeDtypeStruct((m, vw), BF16), jax.ShapeDtypeStruct((batch, heads, dk, dv), F32)],
        scratch_shapes=[pltpu.VMEM((dk, dv), F32)],
        compiler_params=_params("arbitrary", "arbitrary", "arbitrary"),
        name="gla_chunk",
    )(proj, proj, proj, proj, fg, w_fg2, b_fg2, gain, jnp.asarray(coef, BF16), jnp.asarray(mask, F32))


def _gla_step_kernel(qt_ref, kt_ref, fgt_ref, wfg2t_ref, bfgt_ref, v_ref, g_ref, gain_ref, s_ref,
                     o_ref, s_out_ref, *, scale, bt, rank):
    w_t = wfg2t_ref[...]
    fg_t = fgt_ref[...]
    xg = bfgt_ref[...]
    for r in range(rank):
        xg = xg + w_t[:, r:r + 1] * fg_t[r:r + 1, :]
    a_t = jnp.exp(_log_sigmoid(xg) * (1.0 / GATE_TEMP))
    q_t = qt_ref[...] * scale
    k_t = kt_ref[...]
    for j in range(bt):
        s_new = a_t[:, j:j + 1] * s_ref[j] + k_t[:, j:j + 1] * v_ref[j:j + 1, :]
        s_out_ref[j] = s_new
        o = jnp.sum(q_t[:, j:j + 1] * s_new, axis=0, keepdims=True)
        o = o * lax.rsqrt(jnp.mean(o * o, axis=-1, keepdims=True) + EPS) * gain_ref[...]
        o_ref[j:j + 1, :] = o * _silu(g_ref[j:j + 1, :])


def _gla_step(proj, fg, state, w_fg2, b_fg2, gain, heads, dk, dv, rank, bt):
    bs = proj.shape[0]
    kw, vw = heads * dk, heads * dv
    nb = bs // bt
    rank_pad = w_fg2.shape[0]
    to_cols = lambda a: a.reshape(nb, bt, heads, dk).transpose(2, 0, 3, 1)
    q_t = to_cols(proj[:, :kw])
    k_t = to_cols(proj[:, kw:2 * kw])
    fg_t = fg.reshape(nb, bt, rank_pad).transpose(0, 2, 1)
    w_t = w_fg2.reshape(rank_pad, heads, dk).transpose(1, 2, 0)
    b_t = b_fg2.reshape(heads, dk, 1)
    kern = functools.partial(_gla_step_kernel, scale=float(dk) ** -0.5, bt=bt, rank=rank)
    return pl.pallas_call(
        kern,
        grid=(nb, heads),
        in_specs=[
            pl.BlockSpec((None, None, dk, bt), lambda i, h: (h, i, 0, 0)),
            pl.BlockSpec((None, None, dk, bt), lambda i, h: (h, i, 0, 0)),
            pl.BlockSpec((None, rank_pad, bt), lambda i, h: (i, 0, 0)),
            pl.BlockSpec((None, dk, rank_pad), lambda i, h: (h, 0, 0)),
            pl.BlockSpec((None, dk, 1), lambda i, h: (h, 0, 0)),
            pl.BlockSpec((bt, dv), lambda i, h: (i, (2 * kw) // dv + h)),
            pl.BlockSpec((bt, dv), lambda i, h: (i, (2 * kw + vw) // dv + h)),
            pl.BlockSpec((1, dv), lambda i, h: (0, 0)),
            pl.BlockSpec((bt, None, dk, dv), lambda i, h: (i, h, 0, 0)),
        ],
        out_specs=[
            pl.BlockSpec((bt, dv), lambda i, h: (i, h)),
            pl.BlockSpec((bt, None, dk, dv), lambda i, h: (i, h, 0, 0)),
        ],
        out_shape=[jax.ShapeDtypeStruct((bs, vw), F32), jax.ShapeDtypeStruct(state.shape, F32)],
        compiler_params=_params("arbitrary", "arbitrary"),
        name="gla_step",
    )(q_t, k_t, fg_t, w_t, b_t, proj, proj, gain, state)


def _conv_seq_kernel(cb_ref, cc_ref, ch_ref, w_ref, o_ref, st_ref):
    u = cc_ref[...].astype(F32) * ch_ref[...].astype(F32)
    t = u.shape[0]
    rows = lax.broadcasted_iota(jnp.int32, u.shape, 0)
    u1 = jnp.where(rows >= 1, pltpu.roll(u, 1, 0), 0.0)
    u2 = jnp.where(rows >= 2, pltpu.roll(u, 2, 0), 0.0)
    w = w_ref[...]
    conv = w[0:1] * u2 + w[1:2] * u1 + w[2:3] * u
    o_ref[...] = (cb_ref[...].astype(F32) * conv).astype(o_ref.dtype)
    st_ref[...] = u[t - 2:t]


def _conv_seq(proj, conv_w, batch, seq, off, cw):
    assert conv_w.shape[0] == 3 and seq >= 2
    m = proj.shape[0]
    tw = _pick(cw, 512)
    nb = off // tw
    return pl.pallas_call(
        _conv_seq_kernel,
        grid=(batch, cw // tw),
        in_specs=[
            pl.BlockSpec((seq, tw), lambda b, j: (b, nb + j)),
            pl.BlockSpec((seq, tw), lambda b, j: (b, nb + cw // tw + j)),
            pl.BlockSpec((seq, tw), lambda b, j: (b, nb + 2 * (cw // tw) + j)),
            pl.BlockSpec((3, tw), lambda b, j: (0, j)),
        ],
        out_specs=[
            pl.BlockSpec((seq, tw), lambda b, j: (b, j)),
            pl.BlockSpec((None, 2, tw), lambda b, j: (b, 0, j)),
        ],
        out_shape=[jax.ShapeDtypeStruct((m, cw), BF16), jax.ShapeDtypeStruct((batch, 2, cw), F32)],
        compiler_params=_params("arbitrary", "arbitrary"),
        name="conv_seq",
    )(proj, proj, proj, conv_w)


def _conv_step_kernel(cb_ref, cc_ref, ch_ref, s0_ref, s1_ref, w_ref, o_ref, n0_ref, n1_ref):
    u = cc_ref[...] * ch_ref[...]
    w = w_ref[...]
    s1 = s1_ref[...]
    o_ref[...] = cb_ref[...] * (w[0:1] * s0_ref[...] + w[1:2] * s1 + w[2:3] * u)
    n0_ref[...] = s1
    n1_ref[...] = u


def _conv_step(proj, state, conv_w, off, cw):
    assert conv_w.shape[0] == 3 and state.shape[1] == 2
    bs = proj.shape[0]
    tw = _pick(cw, 512)
    nb, nw = off // tw, cw // tw
    st = state.reshape(bs, 2 * cw)
    col = lambda k: pl.BlockSpec((bs, tw), lambda j: (0, k + j))
    o, n0, n1 = pl.pallas_call(
        _conv_step_kernel,
        grid=(nw,),
        in_specs=[col(nb), col(nb + nw), col(nb + 2 * nw), col(0), col(nw), pl.BlockSpec((3, tw), lambda j: (0, j))],
        out_specs=[col(0), col(0), col(0)],
        out_shape=[jax.ShapeDtypeStruct((bs, cw), F32)] * 3,
        compiler_params=_params("arbitrary"),
        name="conv_step",
    )(proj, proj, proj, st, st, conv_w)
    return o, jnp.stack([n0, n1], axis=1)


def _merge_kernel(oa_ref, ob_ref, ga_ref, gb_ref, wa_ref, wb_ref, y_ref):
    ya = _dot(oa_ref[...].astype(BF16), wa_ref[...])
    yb = _dot(ob_ref[...].astype(BF16), wb_ref[...])
    y = jax.nn.sigmoid(ga_ref[...].astype(F32)) * ya + jax.nn.sigmoid(gb_ref[...].astype(F32)) * yb
    y_ref[...] = y.astype(y_ref.dtype)


def _merge(oa, ob, proj, w_pa, w_pb, off_ga, tm):
    m, d = oa.shape[0], w_pa.shape[1]
    tn = _pick(d, 512)
    na, nbk = off_ga // tn, (off_ga + d) // tn
    return pl.pallas_call(
        _merge_kernel,
        grid=(m // tm, d // tn),
        in_specs=[
            pl.BlockSpec((tm, oa.shape[1]), lambda i, j: (i, 0)),
            pl.BlockSpec((tm, ob.shape[1]), lambda i, j: (i, 0)),
            pl.BlockSpec((tm, tn), lambda i, j: (i, na + j)),
            pl.BlockSpec((tm, tn), lambda i, j: (i, nbk + j)),
            pl.BlockSpec((w_pa.shape[0], tn), lambda i, j: (0, j)),
            pl.BlockSpec((w_pb.shape[0], tn), lambda i, j: (0, j)),
        ],
        out_specs=pl.BlockSpec((tm, tn), lambda i, j: (i, j)),
        out_shape=jax.ShapeDtypeStruct((m, d), BF16),
        compiler_params=_params("arbitrary", "arbitrary"),
        name="merge",
    )(oa, ob, proj, proj, w_pa, w_pb)


def _top2_gates(logits, n_experts):
    lane = lax.broadcasted_iota(jnp.int32, logits.shape, 1).astype(F32)
    lg = jnp.where(lane < n_experts, logits, -jnp.inf)
    m1 = jnp.max(lg, axis=1, keepdims=True)
    i1 = jnp.min(jnp.where(lg == m1, lane, float(LANES)), axis=1, keepdims=True)
    lg2 = jnp.where(lane == i1, -jnp.inf, lg)
    m2 = jnp.max(lg2, axis=1, keepdims=True)
    i2 = jnp.min(jnp.where(lg2 == m2, lane, float(LANES)), axis=1, keepdims=True)
    e2 = jnp.exp(m2 - m1)
    den = 1.0 + e2
    return jnp.where(lane == i1, 1.0 / den, 0.0) + jnp.where(lane == i2, e2 / den, 0.0)


def _wo_kernel(y_ref, x_ref, g1_ref, sh_ref, sc_ref, gain_ref, w_ref, *rest, n_experts):
    if n_experts:
        router_ref, x1_ref, h_ref, gates_ref = rest
    else:
        x1_ref, h_ref = rest
    x1 = x_ref[...] + g1_ref[...] * _dot(y_ref[...], w_ref[...])
    x1_ref[...] = x1
    h = _norm_mod(x1, gain_ref[...], sc_ref[...], sh_ref[...])
    h_ref[...] = h.astype(h_ref.dtype)
    if n_experts:
        gates_ref[...] = _top2_gates(_dot_f32(h, router_ref[...]), n_experts)


def _wo(y, x, mod, gain, w_o, router, n_experts, rows_per_group, tm):
    m, d = x.shape
    r = mod.shape[1]
    grp = lambda i: (i * tm) // rows_per_group
    modspec = lambda k: pl.BlockSpec((None, r, d), lambda i: (grp(i), 0, k))
    in_specs = [
        pl.BlockSpec((tm, d), lambda i: (i, 0)),
        pl.BlockSpec((tm, d), lambda i: (i, 0)),
        modspec(2), modspec(3), modspec(4),
        pl.BlockSpec((1, d), lambda i: (0, 0)),
        pl.BlockSpec((d, d), lambda i: (0, 0)),
    ]
    args = [y, x, mod, mod, mod, gain, w_o]
    out_specs = [pl.BlockSpec((tm, d), lambda i: (i, 0)), pl.BlockSpec((tm, d), lambda i: (i, 0))]
    out_shape = [jax.ShapeDtypeStruct((m, d), F32), jax.ShapeDtypeStruct((m, d), BF16)]
    if n_experts:
        in_specs.append(pl.BlockSpec((d, LANES), lambda i: (0, 0)))
        args.append(router)
        out_specs.append(pl.BlockSpec((tm, LANES), lambda i: (i, 0)))
        out_shape.append(jax.ShapeDtypeStruct((m, LANES), F32))
    return pl.pallas_call(
        functools.partial(_wo_kernel, n_experts=n_experts),
        grid=(m // tm,),
        in_specs=in_specs,
        out_specs=out_specs,
        out_shape=out_shape,
        compiler_params=_params("arbitrary"),
        name="wo",
    )(*args)


def _ffn_kernel(h_ref, x_ref, g2_ref, w1_ref, w3_ref, w2_ref, *rest, gated):
    if gated:
        gates_ref, o_ref, acc = rest
    else:
        o_ref, acc = rest
    e, f = pl.program_id(1), pl.program_id(2)

    @pl.when((e == 0) & (f == 0))
    def _():
        acc[...] = jnp.zeros_like(acc)

    h = h_ref[...]
    hid = _silu(_dot(h, w1_ref[...])) * _dot(h, w3_ref[...])
    if gated:
        gates = gates_ref[...]
        lane = lax.broadcasted_iota(jnp.int32, gates.shape, 1)
        hid = hid * jnp.sum(jnp.where(lane == e, gates, 0.0), axis=1, keepdims=True)
    acc[...] += _dot(hid.astype(BF16), w2_ref[...])

    @pl.when((e == pl.num_programs(1) - 1) & (f == pl.num_programs(2) - 1))
    def _():
        o_ref[...] = x_ref[...] + g2_ref[...] * acc[...]


def _ffn(h, x, mod, w1, w3, w2, gates, rows_per_group, tm):
    m, d = x.shape
    ne, _, ff = w1.shape
    r = mod.shape[1]
    tf = _pick(ff, 512)
    grp = lambda i: (i * tm) // rows_per_group
    in_specs = [
        pl.BlockSpec((tm, d), lambda i, e, f: (i, 0)),
        pl.BlockSpec((tm, d), lambda i, e, f: (i, 0)),
        pl.BlockSpec((None, r, d), lambda i, e, f: (grp(i), 0, 5)),
        pl.BlockSpec((None, d, tf), lambda i, e, f: (e, 0, f)),
        pl.BlockSpec((None, d, tf), lambda i, e, f: (e, 0, f)),
        pl.BlockSpec((None, tf, d), lambda i, e, f: (e, f, 0)),
    ]
    args = [h, x, mod, w1, w3, w2]
    if gates is not None:
        in_specs.append(pl.BlockSpec((tm, LANES), lambda i, e, f: (i, 0)))
        args.append(gates)
    return pl.pallas_call(
        functools.partial(_ffn_kernel, gated=gates is not None),
        grid=(m // tm, ne, ff // tf),
        in_specs=in_specs,
        out_specs=pl.BlockSpec((tm, d), lambda i, e, f: (i, 0)),
        out_shape=jax.ShapeDtypeStruct((m, d), F32),
        scratch_shapes=[pltpu.VMEM((tm, d), F32)],
        compiler_params=_params("arbitrary", "arbitrary", "arbitrary"),
        name="ffn",
    )(*args)


def _final_norm_kernel(x_ref, gain_ref, o_ref):
    x = x_ref[...]
    o_ref[...] = x * lax.rsqrt(jnp.mean(x * x, axis=-1, keepdims=True) + EPS) * gain_ref[...]


def _final_norm(x, gain, tm):
    m, d = x.shape
    return pl.pallas_call(
        _final_norm_kernel,
        grid=(m // tm,),
        in_specs=[pl.BlockSpec((tm, d), lambda i: (i, 0)), pl.BlockSpec((1, d), lambda i: (0, 0))],
        out_specs=pl.BlockSpec((tm, d), lambda i: (i, 0)),
        out_shape=jax.ShapeDtypeStruct((m, d), F32),
        compiler_params=_params("arbitrary"),
        name="final_norm",
    )(x, gain)


def _trunk(x, mod, state_gla, state_conv, wts, dims, sample):
    batch, seq, d = x.shape
    heads, dk, dv, cw, n_experts, rank = dims
    kw, vw = heads * dk, heads * dv
    m = batch * seq
    depth = mod.shape[0]
    off_conv = 2 * kw + 2 * vw
    off_ga = off_conv + 3 * cw
    if sample:
        assert seq == 1
        rows_per_group, tm_big, tm_row, act_dtype = m, m, m, F32
    else:
        rows_per_group, act_dtype = seq, BF16
        tm_big, tm_row = _pick(seq, 512), _pick(seq, 512)
        tm_small = _pick(seq, 256)
    xf = x.reshape(m, d)
    new_gla, new_conv = [], []
    for l in range(depth):
        proj, fg = _win(xf, mod[l], wts["norm1"][l], wts["w_main"][l], wts["w_fg"][l], rows_per_group, tm_big, act_dtype)
        if sample:
            oa, sg = _gla_step(proj, fg, state_gla[l], wts["w_fg2"][l], wts["b_fg2"][l], wts["gla_gain"][l],
                               heads, dk, dv, rank, bt=8)
            ob, sc = _conv_step(proj, state_conv[l], wts["conv_w"][l], off_conv, cw)
        else:
            oa, sg = _gla_chunk(proj, fg, wts["w_fg2"][l], wts["b_fg2"][l], wts["gla_gain"][l],
                                batch, seq, heads, dk, dv, chunk=_pick(seq, 256))
            ob, sc = _conv_seq(proj, wts["conv_w"][l], batch, seq, off_conv, cw)
        new_gla.append(sg)
        new_conv.append(sc)
        y = _merge(oa, ob, proj, wts["w_pa"][l], wts["w_pb"][l], off_ga, tm_big)
        if l % 2 == 0:
            x1, h2 = _wo(y, xf, mod[l], wts["norm2"][l], wts["w_o"][l], None, 0, rows_per_group,
                         m if sample else tm_small)
            xf = _ffn(h2, x1, mod[l], wts["dense_w1"][l // 2][None], wts["dense_w3"][l // 2][None],
                      wts["dense_w2"][l // 2][None], None, rows_per_group, tm_row)
        else:
            x1, h2, gates = _wo(y, xf, mod[l], wts["norm2"][l], wts["w_o"][l], wts["router"][l // 2], n_experts,
                                rows_per_group, m if sample else tm_small)
            xf = _ffn(h2, x1, mod[l], wts["moe_w1"][l // 2], wts["moe_w3"][l // 2], wts["moe_w2"][l // 2], gates,
                      rows_per_group, tm_row)
    y = _final_norm(xf, wts["final_norm"], tm_row).reshape(batch, seq, d)
    return y, jnp.stack(new_gla), jnp.stack(new_conv)


def kernel(x_prompt, x_sample, state_gla, state_conv, c_prompt, c_sample, w_ada, b_ada, norm1, norm2, w_in, w_fg2,
           b_fg2, gla_gain, conv_w, w_pa, w_pb, w_o, dense_w1, dense_w3, dense_w2, router, moe_w1, moe_w3, moe_w2,
           final_norm):
    depth, d = norm1.shape
    nb_p, nb_s = x_prompt.shape[0], x_sample.shape[0]
    _, _, heads, dk, dv = state_gla.shape
    kw, vw = heads * dk, heads * dv
    rank = w_fg2.shape[1]
    cw = conv_w.shape[-1]
    n_experts = router.shape[-1]
    assert rank <= LANES and n_experts <= LANES and TOP_K == 2

    fg0 = 2 * kw + 2 * vw
    wts = dict(
        w_main=jnp.concatenate([w_in[:, :, :fg0], w_in[:, :, fg0 + rank:]], axis=-1).astype(BF16),
        w_fg=jnp.pad(w_in[:, :, fg0:fg0 + rank], ((0, 0), (0, 0), (0, LANES - rank))).astype(BF16),
        w_fg2=jnp.pad(w_fg2, ((0, 0), (0, LANES - rank), (0, 0))),
        b_fg2=b_fg2.reshape(depth, 1, kw),
        gla_gain=gla_gain.reshape(depth, 1, dv),
        conv_w=conv_w,
        norm1=norm1.reshape(depth, 1, d),
        norm2=norm2.reshape(depth, 1, d),
        w_pa=w_pa.astype(BF16), w_pb=w_pb.astype(BF16), w_o=w_o.astype(BF16),
        dense_w1=dense_w1.astype(BF16), dense_w3=dense_w3.astype(BF16), dense_w2=dense_w2.astype(BF16),
        router=jnp.pad(router, ((0, 0), (0, 0), (0, LANES - n_experts))),
        moe_w1=moe_w1.astype(BF16), moe_w3=moe_w3.astype(BF16), moe_w2=moe_w2.astype(BF16),
        final_norm=final_norm.reshape(1, d),
    )

    c_all = jnp.concatenate([c_prompt, c_sample], axis=0)
    pad = (-c_all.shape[0]) % 16
    mod = _adaln(jnp.pad(c_all, ((0, pad), (0, 0))), w_ada, b_ada)
    mod_p = mod[:, :nb_p].reshape(depth, nb_p, 1, 6 * d)
    mod_s = mod[:, nb_p:nb_p + nb_s].reshape(depth, 1, nb_s, 6 * d)

    dims = (heads, dk, dv, cw, n_experts, rank)
    zero_gla = None
    y_p, gla_p, conv_p = _trunk(x_prompt, mod_p, zero_gla, None, wts, dims, sample=False)
    y_s, gla_s, conv_s = _trunk(x_sample, mod_s, state_gla, state_conv, wts, dims, sample=True)
    return (y_p, y_s, gla_p, conv_p, gla_s, conv_s)
```

```python
import functools

import numpy as np
import jax
import jax.numpy as jnp
from jax import lax
from jax.experimental import pallas as pl
from jax.experimental.pallas import tpu as pltpu

F32 = jnp.float32
BF16 = jnp.bfloat16
I32 = jnp.int32
EPS = 1e-6
GATE_TEMP = 16.0
TOP_K = 2
LANES = 128
VMEM_LIMIT_BYTES = 56 * 1024 * 1024
MOE_TILE = 512


def _params(*sem):
    return pltpu.CompilerParams(dimension_semantics=sem, vmem_limit_bytes=VMEM_LIMIT_BYTES)


def _dot(a, b):
    return jnp.dot(a, b, preferred_element_type=F32)


def _dot_nt(a, b):
    return lax.dot_general(a, b, (((1,), (1,)), ((), ())), preferred_element_type=F32)


def _split(x):
    hi = x.astype(BF16)
    lo = (x - hi.astype(F32)).astype(BF16)
    return hi, lo


def _dot_f32(a, b):
    ah, al = _split(a)
    bh, bl = _split(b)
    return _dot(ah, bh) + (_dot(ah, bl) + _dot(al, bh))


def _silu(x):
    return x * jax.nn.sigmoid(x)


def _log_sigmoid(x):
    return jnp.minimum(x, 0.0) - jnp.log1p(jnp.exp(-jnp.abs(x)))


def _norm_mod(x, gain, scale, shift):
    y = x * lax.rsqrt(jnp.mean(x * x, axis=-1, keepdims=True) + EPS) * gain
    return y * (1.0 + scale) + shift


def _pick(n, pref):
    t = min(n, pref)
    while n % t:
        t //= 2
    return t


def _mod_spec(mod, l, k, grp, nargs):
    r, d = mod.shape[2], mod.shape[3] // 6
    if nargs == 1:
        return pl.BlockSpec((None, None, r, d), lambda i: (l, grp(i), 0, k))
    return pl.BlockSpec((None, None, r, d), lambda i, j: (l, grp(i), 0, k))


def _adaln_kernel(c_ref, w_ref, b_ref, o_ref):
    a = _silu(c_ref[...]).astype(BF16)
    o_ref[...] = _dot(a, w_ref[...].astype(BF16)) + b_ref[...]


def _adaln(c, w_ada, b_ada):
    depth, d, n = w_ada.shape
    r = c.shape[0]
    tn = _pick(n, 1024)
    return pl.pallas_call(
        _adaln_kernel,
        grid=(depth, n // tn),
        in_specs=[
            pl.BlockSpec((r, d), lambda l, j: (0, 0)),
            pl.BlockSpec((None, d, tn), lambda l, j: (l, 0, j)),
            pl.BlockSpec((None, 1, tn), lambda l, j: (l, 0, j)),
        ],
        out_specs=pl.BlockSpec((None, r, tn), lambda l, j: (l, 0, j)),
        out_shape=jax.ShapeDtypeStruct((depth, r, n), F32),
        compiler_params=_params("arbitrary", "arbitrary"),
        name="adaln",
    )(c, w_ada, b_ada.reshape(depth, 1, n))


def _win_kernel(x_ref, sh_ref, sc_ref, gain_ref, w_ref, wfg_ref, proj_ref, fg_ref, h_scr):
    @pl.when(pl.program_id(1) == 0)
    def _():
        h = _norm_mod(x_ref[...], gain_ref[...], sc_ref[...], sh_ref[...]).astype(BF16)
        h_scr[...] = h
        fg_ref[...] = _dot(h, wfg_ref[...])

    proj_ref[...] = _dot(h_scr[...], w_ref[...]).astype(proj_ref.dtype)


def _win(x, mod, l, gain, w_main, w_fg, rows_per_group, tm, out_dtype):
    m, d = x.shape
    n = w_main.shape[2]
    tn = _pick(n, 1024)
    grp = lambda i: (i * tm) // rows_per_group
    return pl.pallas_call(
        _win_kernel,
        grid=(m // tm, n // tn),
        in_specs=[
            pl.BlockSpec((tm, d), lambda i, j: (i, 0)),
            _mod_spec(mod, l, 0, grp, 2),
            _mod_spec(mod, l, 1, grp, 2),
            pl.BlockSpec((None, 1, d), lambda i, j: (l, 0, 0)),
            pl.BlockSpec((None, d, tn), lambda i, j: (l, 0, j)),
            pl.BlockSpec((None, d, LANES), lambda i, j: (l, 0, 0)),
        ],
        out_specs=[
            pl.BlockSpec((tm, tn), lambda i, j: (i, j)),
            pl.BlockSpec((tm, LANES), lambda i, j: (i, 0)),
        ],
        out_shape=[jax.ShapeDtypeStruct((m, n), out_dtype), jax.ShapeDtypeStruct((m, LANES), F32)],
        scratch_shapes=[pltpu.VMEM((tm, d), BF16)],
        compiler_params=_params("arbitrary", "arbitrary"),
        name="win",
    )(x, mod, mod, gain, w_main, w_fg)


def _gla_tables(c):
    i = np.arange(c)[:, None]
    j = np.arange(c)[None, :]
    coefs = [(j <= i), (j > i)]
    masks = [(i == j)]
    b = c
    while b >= 2:
        mid = (i // b) * b + b // 2 - 1
        coefs.append(((j > mid) & (j <= i)) | ((j > i) & (j <= mid)))
        masks.append((i // b == j // b) & (i % b >= b // 2) & (j % b < b // 2))
        b //= 2
    return np.concatenate(coefs, 0).astype(np.float32), np.stack(masks).astype(np.float32)


def _gla_chunk_kernel(q_ref, k_ref, v_ref, g_ref, fg_ref, wfg2_ref, bfg_ref, gain_ref, coef_ref, mask_ref,
                      o_ref, s_out_ref, s_scr, *, scale, chunk, levels):
    ci = pl.program_id(2)

    @pl.when(ci == 0)
    def _():
        s_scr[...] = jnp.zeros_like(s_scr)

    c = chunk
    log_a = _log_sigmoid(_dot_f32(fg_ref[...], wfg2_ref[...]) + bfg_ref[...]) * (1.0 / GATE_TEMP)
    la_hi, la_lo = _split(log_a)
    coef = coef_ref[...]
    decay = jnp.exp(_dot(coef, la_hi) + _dot(coef, la_lo))

    q = q_ref[...].astype(F32) * scale
    k = k_ref[...].astype(F32)
    v = v_ref[...].astype(BF16)
    d_cum = decay[0:c]
    d_tail = decay[c:2 * c]

    scores = mask_ref[0] * _dot_nt(q.astype(BF16), k.astype(BF16))
    for lv in range(levels):
        d_lv = decay[(2 + lv) * c:(3 + lv) * c]
        scores = scores + mask_ref[1 + lv] * _dot_nt((q * d_lv).astype(BF16), (k * d_lv).astype(BF16))

    s = s_scr[...]
    o = _dot(scores.astype(BF16), v) + _dot((q * d_cum).astype(BF16), s.astype(BF16))

    nt = min(c, LANES)
    a_col = jnp.transpose(d_cum[c - nt:c])[:, nt - 1:nt]
    k_tail_t = jnp.transpose(k * d_tail).astype(BF16)
    s_new = a_col * s + _dot(k_tail_t, v)
    s_scr[...] = s_new

    o = o * lax.rsqrt(jnp.mean(o * o, axis=-1, keepdims=True) + EPS) * gain_ref[...]
    o_ref[...] = (o * _silu(g_ref[...].astype(F32))).astype(o_ref.dtype)

    @pl.when(ci == pl.num_programs(2) - 1)
    def _():
        s_out_ref[...] = s_new


def _gla_chunk(proj, fg, l, w_fg2, b_fg2, gain, batch, seq, heads, dk, dv, chunk):
    m = proj.shape[0]
    kw, vw = heads * dk, heads * dv
    nc = seq // chunk
    coef, mask = _gla_tables(chunk)
    levels = mask.shape[0] - 1
    rank_pad = w_fg2.shape[1]
    row = lambda b, h, c: b * nc + c
    kern = functools.partial(_gla_chunk_kernel, scale=float(dk) ** -0.5, chunk=chunk, levels=levels)
    return pl.pallas_call(
        kern,
        grid=(batch, heads, nc),
        in_specs=[
            pl.BlockSpec((chunk, dk), lambda b, h, c: (row(b, h, c), h)),
            pl.BlockSpec((chunk, dk), lambda b, h, c: (row(b, h, c), kw // dk + h)),
            pl.BlockSpec((chunk, dv), lambda b, h, c: (row(b, h, c), (2 * kw) // dv + h)),
            pl.BlockSpec((chunk, dv), lambda b, h, c: (row(b, h, c), (2 * kw + vw) // dv + h)),
            pl.BlockSpec((chunk, LANES), lambda b, h, c: (row(b, h, c), 0)),
            pl.BlockSpec((None, rank_pad, dk), lambda b, h, c: (l, 0, h)),
            pl.BlockSpec((None, 1, dk), lambda b, h, c: (l, 0, h)),
            pl.BlockSpec((None, 1, dv), lambda b, h, c: (l, 0, 0)),
            pl.BlockSpec(coef.shape, lambda b, h, c: (0, 0)),
            pl.BlockSpec(mask.shape, lambda b, h, c: (0, 0, 0)),
        ],
        out_specs=[
            pl.BlockSpec((chunk, dv), lambda b, h, c: (row(b, h, c), h)),
            pl.BlockSpec((None, None, dk, dv), lambda b, h, c: (b, h, 0, 0)),
        ],
        out_shape=[jax.ShapeDtypeStruct((m, vw), BF16), jax.ShapeDtypeStruct((batch, heads, dk, dv), F32)],
        scratch_shapes=[pltpu.VMEM((dk, dv), F32)],
        compiler_params=_params("arbitrary", "arbitrary", "arbitrary"),
        name="gla_chunk",
    )(proj, proj, proj, proj, fg, w_fg2, b_fg2, gain, jnp.asarray(coef, BF16), jnp.asarray(mask, F32))


def _gla_step_kernel(qt_ref, kt_ref, fgt_ref, wfg2t_ref, bfgt_ref, v_ref, g_ref, gain_ref, s_ref, carry_ref,
                     o_ref, s_out_ref, *, scale, bt, rank):
    del carry_ref
    w_t = wfg2t_ref[...]
    fg_t = fgt_ref[...]
    xg = bfgt_ref[...]
    for r in range(rank):
        xg = xg + w_t[:, r:r + 1] * fg_t[r:r + 1, :]
    a_t = jnp.exp(_log_sigmoid(xg) * (1.0 / GATE_TEMP))
    q_t = qt_ref[...] * scale
    k_t = kt_ref[...]
    for j in range(bt):
        s_new = a_t[:, j:j + 1] * s_ref[j] + k_t[:, j:j + 1] * v_ref[j:j + 1, :]
        s_out_ref[j] = s_new
        o = jnp.sum(q_t[:, j:j + 1] * s_new, axis=0, keepdims=True)
        o = o * lax.rsqrt(jnp.mean(o * o, axis=-1, keepdims=True) + EPS) * gain_ref[...]
        o_ref[j:j + 1, :] = o * _silu(g_ref[j:j + 1, :])


def _gla_step(proj, fg, state, new_state, l, w_fg2, b_fg2, gain, heads, dk, dv, rank, bt):
    bs = proj.shape[0]
    kw, vw = heads * dk, heads * dv
    nb = bs // bt
    rank_pad = w_fg2.shape[1]
    to_cols = lambda a: a.reshape(nb, bt, heads, dk).transpose(2, 0, 3, 1)
    q_t = to_cols(proj[:, :kw])
    k_t = to_cols(proj[:, kw:2 * kw])
    fg_t = fg.reshape(nb, bt, rank_pad).transpose(0, 2, 1)
    w_t = w_fg2[l].reshape(rank_pad, heads, dk).transpose(1, 2, 0)
    b_t = b_fg2[l].reshape(heads, dk, 1)
    kern = functools.partial(_gla_step_kernel, scale=float(dk) ** -0.5, bt=bt, rank=rank)
    state_spec = pl.BlockSpec((None, bt, None, dk, dv), lambda i, h: (l, i, h, 0, 0))
    in_specs = [
        pl.BlockSpec((None, None, dk, bt), lambda i, h: (h, i, 0, 0)),
        pl.BlockSpec((None, None, dk, bt), lambda i, h: (h, i, 0, 0)),
        pl.BlockSpec((None, rank_pad, bt), lambda i, h: (i, 0, 0)),
        pl.BlockSpec((None, dk, rank_pad), lambda i, h: (h, 0, 0)),
        pl.BlockSpec((None, dk, 1), lambda i, h: (h, 0, 0)),
        pl.BlockSpec((bt, dv), lambda i, h: (i, (2 * kw) // dv + h)),
        pl.BlockSpec((bt, dv), lambda i, h: (i, (2 * kw + vw) // dv + h)),
        pl.BlockSpec((None, 1, dv), lambda i, h: (l, 0, 0)),
        state_spec,
        pl.BlockSpec(memory_space=pl.ANY),
    ]
    args = [q_t, k_t, fg_t, w_t, b_t, proj, proj, gain, state, new_state]
    return pl.pallas_call(
        kern,
        grid=(nb, heads),
        in_specs=in_specs,
        out_specs=[pl.BlockSpec((bt, dv), lambda i, h: (i, h)), state_spec],
        out_shape=[jax.ShapeDtypeStruct((bs, vw), F32), jax.ShapeDtypeStruct(state.shape, F32)],
        input_output_aliases={len(args) - 1: 1},
        compiler_params=_params("arbitrary", "arbitrary"),
        name="gla_step",
    )(*args)


def _conv_seq_kernel(cb_ref, cc_ref, ch_ref, w_ref, o_ref, st_ref):
    u = cc_ref[...].astype(F32) * ch_ref[...].astype(F32)
    t = u.shape[0]
    rows = lax.broadcasted_iota(jnp.int32, u.shape, 0)
    u1 = jnp.where(rows >= 1, pltpu.roll(u, 1, 0), 0.0)
    u2 = jnp.where(rows >= 2, pltpu.roll(u, 2, 0), 0.0)
    w = w_ref[...]
    conv = w[0:1] * u2 + w[1:2] * u1 + w[2:3] * u
    o_ref[...] = (cb_ref[...].astype(F32) * conv).astype(o_ref.dtype)
    st_ref[...] = u[t - 2:t]


def _conv_seq(proj, l, conv_w, batch, seq, off, cw):
    assert conv_w.shape[1] == 3 and seq >= 2
    m = proj.shape[0]
    tw = _pick(cw, 512)
    nb = off // tw
    return pl.pallas_call(
        _conv_seq_kernel,
        grid=(batch, cw // tw),
        in_specs=[
            pl.BlockSpec((seq, tw), lambda b, j: (b, nb + j)),
            pl.BlockSpec((seq, tw), lambda b, j: (b, nb + cw // tw + j)),
            pl.BlockSpec((seq, tw), lambda b, j: (b, nb + 2 * (cw // tw) + j)),
            pl.BlockSpec((None, 3, tw), lambda b, j: (l, 0, j)),
        ],
        out_specs=[
            pl.BlockSpec((seq, tw), lambda b, j: (b, j)),
            pl.BlockSpec((None, 2, tw), lambda b, j: (b, 0, j)),
        ],
        out_shape=[jax.ShapeDtypeStruct((m, cw), BF16), jax.ShapeDtypeStruct((batch, 2, cw), F32)],
        compiler_params=_params("arbitrary", "arbitrary"),
        name="conv_seq",
    )(proj, proj, proj, conv_w)


def _conv_step_kernel(cb_ref, cc_ref, ch_ref, s0_ref, s1_ref, w_ref, o_ref, n0_ref, n1_ref):
    u = cc_ref[...] * ch_ref[...]
    w = w_ref[...]
    s1 = s1_ref[...]
    o_ref[...] = cb_ref[...] * (w[0:1] * s0_ref[...] + w[1:2] * s1 + w[2:3] * u)
    n0_ref[...] = s1
    n1_ref[...] = u


def _conv_step(proj, state, l, conv_w, off, cw):
    assert conv_w.shape[1] == 3 and state.shape[2] == 2
    bs = proj.shape[0]
    tw = _pick(cw, 512)
    nb, nw = off // tw, cw // tw
    st = state.reshape(state.shape[0], bs, 2 * cw)
    col = lambda k: pl.BlockSpec((bs, tw), lambda j: (0, k + j))
    stc = lambda k: pl.BlockSpec((None, bs, tw), lambda j: (l, 0, k + j))
    o, n0, n1 = pl.pallas_call(
        _conv_step_kernel,
        grid=(nw,),
        in_specs=[col(nb), col(nb + nw), col(nb + 2 * nw), stc(0), stc(nw),
                  pl.BlockSpec((None, 3, tw), lambda j: (l, 0, j))],
        out_specs=[col(0), col(0), col(0)],
        out_shape=[jax.ShapeDtypeStruct((bs, cw), F32)] * 3,
        compiler_params=_params("arbitrary"),
        name="conv_step",
    )(proj, proj, proj, st, st, conv_w)
    return o, jnp.stack([n0, n1], axis=1)


def _merge_kernel(oa_ref, ob_ref, ga_ref, gb_ref, wa_ref, wb_ref, y_ref):
    ya = _dot(oa_ref[...].astype(BF16), wa_ref[...])
    yb = _dot(ob_ref[...].astype(BF16), wb_ref[...])
    y = jax.nn.sigmoid(ga_ref[...].astype(F32)) * ya + jax.nn.sigmoid(gb_ref[...].astype(F32)) * yb
    y_ref[...] = y.astype(y_ref.dtype)


def _merge(oa, ob, proj, l, w_pa, w_pb, off_ga, tm):
    m, d = oa.shape[0], w_pa.shape[2]
    tn = _pick(d, 512)
    na, nbk = off_ga // tn, (off_ga + d) // tn
    return pl.pallas_call(
        _merge_kernel,
        grid=(m // tm, d // tn),
        in_specs=[
            pl.BlockSpec((tm, oa.shape[1]), lambda i, j: (i, 0)),
            pl.BlockSpec((tm, ob.shape[1]), lambda i, j: (i, 0)),
            pl.BlockSpec((tm, tn), lambda i, j: (i, na + j)),
            pl.BlockSpec((tm, tn), lambda i, j: (i, nbk + j)),
            pl.BlockSpec((None, w_pa.shape[1], tn), lambda i, j: (l, 0, j)),
            pl.BlockSpec((None, w_pb.shape[1], tn), lambda i, j: (l, 0, j)),
        ],
        out_specs=pl.BlockSpec((tm, tn), lambda i, j: (i, j)),
        out_shape=jax.ShapeDtypeStruct((m, d), BF16),
        compiler_params=_params("arbitrary", "arbitrary"),
        name="merge",
    )(oa, ob, proj, proj, w_pa, w_pb)


def _top2_route(logits, n_experts):
    lane = lax.broadcasted_iota(jnp.int32, logits.shape, 1).astype(F32)
    lg = jnp.where(lane < n_experts, logits, -jnp.inf)
    m1 = jnp.max(lg, axis=1, keepdims=True)
    i1 = jnp.min(jnp.where(lg == m1, lane, float(LANES)), axis=1, keepdims=True)
    lg2 = jnp.where(lane == i1, -jnp.inf, lg)
    m2 = jnp.max(lg2, axis=1, keepdims=True)
    i2 = jnp.min(jnp.where(lg2 == m2, lane, float(LANES)), axis=1, keepdims=True)
    e2 = jnp.exp(m2 - m1)
    den = 1.0 + e2
    return jnp.where(lane == 0.0, i1, jnp.where(lane == 1.0, i2, jnp.where(lane == 2.0, 1.0 / den, e2 / den)))


def _wo_kernel(y_ref, x_ref, g1_ref, sh_ref, sc_ref, gain_ref, w_ref, *rest, n_experts):
    if n_experts:
        router_ref, x1_ref, h_ref, route_ref = rest
    else:
        x1_ref, h_ref = rest
    x1 = x_ref[...] + g1_ref[...] * _dot(y_ref[...], w_ref[...])
    x1_ref[...] = x1
    h = _norm_mod(x1, gain_ref[...], sc_ref[...], sh_ref[...])
    h_ref[...] = h.astype(h_ref.dtype)
    if n_experts:
        route_ref[...] = _top2_route(_dot_f32(h, router_ref[...]), n_experts)


def _wo(y, x, mod, l, gain, w_o, router, lm, n_experts, rows_per_group, tm):
    m, d = x.shape
    grp = lambda i: (i * tm) // rows_per_group
    in_specs = [
        pl.BlockSpec((tm, d), lambda i: (i, 0)),
        pl.BlockSpec((tm, d), lambda i: (i, 0)),
        _mod_spec(mod, l, 2, grp, 1), _mod_spec(mod, l, 3, grp, 1), _mod_spec(mod, l, 4, grp, 1),
        pl.BlockSpec((None, 1, d), lambda i: (l, 0, 0)),
        pl.BlockSpec((None, d, d), lambda i: (l, 0, 0)),
    ]
    args = [y, x, mod, mod, mod, gain, w_o]
    out_specs = [pl.BlockSpec((tm, d), lambda i: (i, 0)), pl.BlockSpec((tm, d), lambda i: (i, 0))]
    out_shape = [jax.ShapeDtypeStruct((m, d), F32), jax.ShapeDtypeStruct((m, d), F32 if n_experts else BF16)]
    if n_experts:
        in_specs.append(pl.BlockSpec((None, d, LANES), lambda i: (lm, 0, 0)))
        args.append(router)
        out_specs.append(pl.BlockSpec((tm, LANES), lambda i: (i, 0)))
        out_shape.append(jax.ShapeDtypeStruct((m, LANES), F32))
    return pl.pallas_call(
        functools.partial(_wo_kernel, n_experts=n_experts),
        grid=(m // tm,),
        in_specs=in_specs,
        out_specs=out_specs,
        out_shape=out_shape,
        compiler_params=_params("arbitrary"),
        name="wo",
    )(*args)


def _ffn_kernel(h_ref, x_ref, g2_ref, w1_ref, w3_ref, w2_ref, o_ref, acc):
    f = pl.program_id(1)

    @pl.when(f == 0)
    def _():
        acc[...] = jnp.zeros_like(acc)

    h = h_ref[...]
    hid = _silu(_dot(h, w1_ref[...])) * _dot(h, w3_ref[...])
    acc[...] += _dot(hid.astype(BF16), w2_ref[...])

    @pl.when(f == pl.num_programs(1) - 1)
    def _():
        o_ref[...] = x_ref[...] + g2_ref[...] * acc[...]


def _ffn(h, x, mod, l, ld, w1, w3, w2, rows_per_group, tm):
    m, d = x.shape
    ff = w1.shape[2]
    tf = _pick(ff, 512)
    grp = lambda i: (i * tm) // rows_per_group
    return pl.pallas_call(
        _ffn_kernel,
        grid=(m // tm, ff // tf),
        in_specs=[
            pl.BlockSpec((tm, d), lambda i, f: (i, 0)),
            pl.BlockSpec((tm, d), lambda i, f: (i, 0)),
            _mod_spec(mod, l, 5, grp, 2),
            pl.BlockSpec((None, d, tf), lambda i, f: (ld, 0, f)),
            pl.BlockSpec((None, d, tf), lambda i, f: (ld, 0, f)),
            pl.BlockSpec((None, tf, d), lambda i, f: (ld, f, 0)),
        ],
        out_specs=pl.BlockSpec((tm, d), lambda i, f: (i, 0)),
        out_shape=jax.ShapeDtypeStruct((m, d), F32),
        scratch_shapes=[pltpu.VMEM((tm, d), F32)],
        compiler_params=_params("arbitrary", "arbitrary"),
        name="ffn",
    )(h, x, mod, w1, w3, w2)


def _route_tables(route, n_experts, tm):
    m = route.shape[0]
    p_rows = -(-(TOP_K * m + n_experts * (tm - 1)) // tm) * tm
    e_flat = jnp.concatenate([route[:, 0], route[:, 1]]).astype(I32)
    onehot = (e_flat[:, None] == jnp.arange(n_experts, dtype=I32)[None, :]).astype(I32)
    rank = jnp.sum((jnp.cumsum(onehot, axis=0) - 1) * onehot, axis=1)
    counts = jnp.sum(onehot, axis=0)
    padded = ((counts + tm - 1) // tm) * tm
    ends = jnp.cumsum(padded)
    dest = (jnp.sum(onehot * (ends - padded)[None, :], axis=1) + rank).astype(I32)
    tile_start = jnp.arange(p_rows // tm, dtype=I32) * tm
    tile_expert = jnp.minimum(jnp.sum((tile_start[:, None] >= ends[None, :]).astype(I32), axis=1), n_experts - 1)
    n_used = (ends[-1] // tm).astype(I32).reshape(1)
    src = jnp.zeros((p_rows,), I32).at[dest].set(jnp.tile(jnp.arange(m, dtype=I32), TOP_K))
    return src, tile_expert.astype(I32), n_used, dest


def _row_copy(table_ref, row, dst_ref, r, sem):
    return pltpu.make_async_copy(table_ref.at[pl.ds(row, 1), :], dst_ref.at[pl.ds(r, 1), :], sem)


def _gather_rows(idx_ref, base, n, table_ref, dst_ref, sem):
    def start(r, carry):
        _row_copy(table_ref, idx_ref[base + r], dst_ref, r, sem).start()
        return carry

    def wait(r, carry):
        _row_copy(table_ref, 0, dst_ref, r, sem).wait()
        return carry

    lax.fori_loop(0, n, start, 0, unroll=8)
    lax.fori_loop(0, n, wait, 0, unroll=8)


def _dispatch_kernel(src_ref, nu_ref, h_hbm, o_ref, sem):
    t = pl.program_id(0)
    tm = o_ref.shape[0]

    @pl.when(t < nu_ref[0])
    def _():
        _gather_rows(src_ref, t * tm, tm, h_hbm, o_ref, sem)

    @pl.when(t >= nu_ref[0])
    def _():
        o_ref[...] = jnp.zeros_like(o_ref)


def _dispatch(h, src, n_used, tm):
    d = h.shape[1]
    p_rows = src.shape[0]
    return pl.pallas_call(
        _dispatch_kernel,
        grid_spec=pltpu.PrefetchScalarGridSpec(
            num_scalar_prefetch=2,
            grid=(p_rows // tm,),
            in_specs=[pl.BlockSpec(memory_space=pl.ANY)],
            out_specs=pl.BlockSpec((tm, d), lambda t, src, nu: (t, 0)),
            scratch_shapes=[pltpu.SemaphoreType.DMA(())],
        ),
        out_shape=jax.ShapeDtypeStruct((p_rows, d), h.dtype),
        compiler_params=_params("arbitrary"),
        name="moe_dispatch",
    )(src, n_used, h)


def _ffn_grouped_kernel(te_ref, nu_ref, x_ref, w1_ref, w3_ref, w2_ref, o_ref, xb, acc):
    t, f = pl.program_id(0), pl.program_id(1)
    used = t < nu_ref[0]
    last = f == pl.num_programs(1) - 1

    @pl.when(used)
    def _():
        @pl.when(f == 0)
        def _():
            xb[...] = x_ref[...].astype(BF16)
            acc[...] = jnp.zeros_like(acc)

        x = xb[...]
        hid = _silu(_dot(x, w1_ref[...])) * _dot(x, w3_ref[...])
        acc[...] += _dot(hid.astype(BF16), w2_ref[...])

    @pl.when(used & last)
    def _():
        o_ref[...] = acc[...]

    @pl.when(jnp.logical_not(used) & last)
    def _():
        o_ref[...] = jnp.zeros_like(o_ref)


def _ffn_grouped(xs, tile_expert, n_used, lm, w1, w3, w2, tm):
    p_rows, d = xs.shape
    ff = w1.shape[3]
    tf = _pick(ff, 512)
    nf = ff // tf
    fidx = lambda t, f, nu: jnp.where(t < nu[0], f, nf - 1)
    return pl.pallas_call(
        _ffn_grouped_kernel,
        grid_spec=pltpu.PrefetchScalarGridSpec(
            num_scalar_prefetch=2,
            grid=(p_rows // tm, nf),
            in_specs=[
                pl.BlockSpec((tm, d), lambda t, f, te, nu: (t, 0)),
                pl.BlockSpec((None, None, d, tf), lambda t, f, te, nu: (lm, te[t], 0, fidx(t, f, nu))),
                pl.BlockSpec((None, None, d, tf), lambda t, f, te, nu: (lm, te[t], 0, fidx(t, f, nu))),
                pl.BlockSpec((None, None, tf, d), lambda t, f, te, nu: (lm, te[t], fidx(t, f, nu), 0)),
            ],
            out_specs=pl.BlockSpec((tm, d), lambda t, f, te, nu: (t, 0)),
            scratch_shapes=[pltpu.VMEM((tm, d), BF16), pltpu.VMEM((tm, d), F32)],
        ),
        out_shape=jax.ShapeDtypeStruct((p_rows, d), F32),
        compiler_params=_params("arbitrary", "arbitrary"),
        name="moe_ffn",
    )(tile_expert, n_used, xs, w1, w3, w2)


def _combine_kernel(dest_ref, x_ref, g2_ref, route_ref, ys_hbm, o_ref, ybuf, sem, *, row0, m_all):
    tm = x_ref.shape[0]
    base = row0 + pl.program_id(0) * tm
    for slot in range(TOP_K):
        _gather_rows(dest_ref, slot * m_all + base, tm, ys_hbm, ybuf.at[slot], sem)
    route = route_ref[...]
    f = route[:, 2:3] * ybuf[0] + route[:, 3:4] * ybuf[1]
    o_ref[...] = x_ref[...] + g2_ref[...] * f


def _combine(x1, mod, l, route, ys, dest, row0, m_all, rows_per_group, tm):
    m, d = x1.shape
    grp = lambda i: (i * tm) // rows_per_group
    r = mod.shape[2]
    return pl.pallas_call(
        functools.partial(_combine_kernel, row0=row0, m_all=m_all),
        grid_spec=pltpu.PrefetchScalarGridSpec(
            num_scalar_prefetch=1,
            grid=(m // tm,),
            in_specs=[
                pl.BlockSpec((tm, d), lambda i, dest: (i, 0)),
                pl.BlockSpec((None, None, r, d), lambda i, dest: (l, grp(i), 0, 5)),
                pl.BlockSpec((tm, LANES), lambda i, dest: (i, 0)),
                pl.BlockSpec(memory_space=pl.ANY),
            ],
            out_specs=pl.BlockSpec((tm, d), lambda i, dest: (i, 0)),
            scratch_shapes=[pltpu.VMEM((TOP_K, tm, d), F32), pltpu.SemaphoreType.DMA(())],
        ),
        out_shape=jax.ShapeDtypeStruct((m, d), F32),
        compiler_params=_params("arbitrary"),
        name="moe_combine",
    )(dest, x1, mod, route, ys)


def _final_norm_kernel(x_ref, gain_ref, o_ref):
    x = x_ref[...]
    o_ref[...] = x * lax.rsqrt(jnp.mean(x * x, axis=-1, keepdims=True) + EPS) * gain_ref[...]


def _final_norm(x, gain, tm):
    m, d = x.shape
    return pl.pallas_call(
        _final_norm_kernel,
        grid=(m // tm,),
        in_specs=[pl.BlockSpec((tm, d), lambda i: (i, 0)), pl.BlockSpec((1, d), lambda i: (0, 0))],
        out_specs=pl.BlockSpec((tm, d), lambda i: (i, 0)),
        out_shape=jax.ShapeDtypeStruct((m, d), F32),
        compiler_params=_params("arbitrary"),
        name="final_norm",
    )(x, gain)


def kernel(x_prompt, x_sample, state_gla, state_conv, c_prompt, c_sample, w_ada, b_ada, norm1, norm2, w_in, w_fg2,
           b_fg2, gla_gain, conv_w, w_pa, w_pb, w_o, dense_w1, dense_w3, dense_w2, router, moe_w1, moe_w3, moe_w2,
           final_norm):
    depth, d = norm1.shape
    nb_p, seq, _ = x_prompt.shape
    nb_s = x_sample.shape[0]
    assert x_sample.shape[1] == 1
    _, _, heads, dk, dv = state_gla.shape
    kw, vw = heads * dk, heads * dv
    rank = w_fg2.shape[1]
    cw = conv_w.shape[-1]
    n_experts = router.shape[-1]
    assert rank <= LANES and n_experts <= LANES and TOP_K == 2
    m_p, m_s = nb_p * seq, nb_s
    off_conv = 2 * kw + 2 * vw
    off_ga = off_conv + 3 * cw

    w_main = jnp.concatenate([w_in[:, :, :off_conv], w_in[:, :, off_conv + rank:]], axis=-1).astype(BF16)
    w_fg = jnp.pad(w_in[:, :, off_conv:off_conv + rank], ((0, 0), (0, 0), (0, LANES - rank))).astype(BF16)
    w_fg2p = jnp.pad(w_fg2, ((0, 0), (0, LANES - rank), (0, 0)))
    b_fg2r = b_fg2.reshape(depth, 1, kw)
    gain_r = gla_gain.reshape(depth, 1, dv)
    norm1r, norm2r = norm1.reshape(depth, 1, d), norm2.reshape(depth, 1, d)
    w_pab, w_pbb, w_ob = w_pa.astype(BF16), w_pb.astype(BF16), w_o.astype(BF16)
    d_w1, d_w3, d_w2 = dense_w1.astype(BF16), dense_w3.astype(BF16), dense_w2.astype(BF16)
    m_w1, m_w3, m_w2 = moe_w1.astype(BF16), moe_w3.astype(BF16), moe_w2.astype(BF16)
    router_p = jnp.pad(router, ((0, 0), (0, 0), (0, LANES - n_experts)))

    c_all = jnp.concatenate([c_prompt, c_sample], axis=0)
    mod = _adaln(jnp.pad(c_all, ((0, (-c_all.shape[0]) % 16), (0, 0))), w_ada, b_ada)
    mod_p = mod[:, :nb_p].reshape(depth, nb_p, 1, 6 * d)
    mod_s = mod[:, nb_p:nb_p + nb_s].reshape(depth, 1, nb_s, 6 * d)

    tm_p = _pick(seq, 512)
    tm_wo = _pick(seq, 256)
    xp, xs = x_prompt.reshape(m_p, d), x_sample.reshape(m_s, d)
    gla_p, conv_p, conv_s = [], [], []
    gla_s = jnp.zeros(state_gla.shape, F32)
    for l in range(depth):
        proj_p, fg_p = _win(xp, mod_p, l, norm1r, w_main, w_fg, seq, tm_p, BF16)
        proj_s, fg_s = _win(xs, mod_s, l, norm1r, w_main, w_fg, m_s, m_s, F32)
        oa_p, sg = _gla_chunk(proj_p, fg_p, l, w_fg2p, b_fg2r, gain_r, nb_p, seq, heads, dk, dv, _pick(seq, 256))
        gla_p.append(sg)
        oa_s, gla_s = _gla_step(proj_s, fg_s, state_gla, gla_s, l, w_fg2p, b_fg2r, gain_r, heads, dk, dv, rank, bt=8)
        ob_p, sc = _conv_seq(proj_p, l, conv_w, nb_p, seq, off_conv, cw)
        conv_p.append(sc)
        ob_s, sc = _conv_step(proj_s, state_conv, l, conv_w, off_conv, cw)
        conv_s.append(sc)
        y_p = _merge(oa_p, ob_p, proj_p, l, w_pab, w_pbb, off_ga, tm_p)
        y_s = _merge(oa_s, ob_s, proj_s, l, w_pab, w_pbb, off_ga, m_s)
        if l % 2 == 0:
            x1_p, h_p = _wo(y_p, xp, mod_p, l, norm2r, w_ob, None, 0, 0, seq, tm_wo)
            x1_s, h_s = _wo(y_s, xs, mod_s, l, norm2r, w_ob, None, 0, 0, m_s, m_s)
            xp = _ffn(h_p, x1_p, mod_p, l, l // 2, d_w1, d_w3, d_w2, seq, tm_p)
            xs = _ffn(h_s, x1_s, mod_s, l, l // 2, d_w1, d_w3, d_w2, m_s, m_s)
        else:
            lm = l // 2
            x1_p, h_p, route_p = _wo(y_p, xp, mod_p, l, norm2r, w_ob, router_p, lm, n_experts, seq, tm_wo)
            x1_s, h_s, route_s = _wo(y_s, xs, mod_s, l, norm2r, w_ob, router_p, lm, n_experts, m_s, m_s)
            h_all = jnp.concatenate([h_p, h_s], axis=0)
            src, tile_expert, n_used, dest = _route_tables(jnp.concatenate([route_p, route_s], axis=0),
                                                           n_experts, MOE_TILE)
            ys = _ffn_grouped(_dispatch(h_all, src, n_used, MOE_TILE), tile_expert, n_used, lm, m_w1, m_w3, m_w2,
                              MOE_TILE)
            xp = _combine(x1_p, mod_p, l, route_p, ys, dest, 0, m_p + m_s, seq, tm_wo)
            xs = _combine(x1_s, mod_s, l, route_s, ys, dest, m_p, m_p + m_s, m_s, m_s)
    y_p = _final_norm(xp, final_norm.reshape(1, d), tm_p).reshape(nb_p, seq, d)
    y_s = _final_norm(xs, final_norm.reshape(1, d), m_s).reshape(nb_s, 1, d)
    return (y_p, y_s, jnp.stack(gla_p), jnp.stack(conv_p), gla_s, jnp.stack(conv_s))
```

```python
import functools

import numpy as np
import jax
import jax.numpy as jnp
from jax import lax
from jax.experimental import pallas as pl
from jax.experimental.pallas import tpu as pltpu

F32 = jnp.float32
BF16 = jnp.bfloat16
I32 = jnp.int32
EPS = 1e-6
GATE_TEMP = 16.0
TOP_K = 2
LANES = 128
VMEM_LIMIT_BYTES = 56 * 1024 * 1024
MOE_TILE = 512


def _params(*sem):
    return pltpu.CompilerParams(dimension_semantics=sem, vmem_limit_bytes=VMEM_LIMIT_BYTES)


def _dot(a, b):
    return jnp.dot(a, b, preferred_element_type=F32)


def _dot_nt(a, b):
    return lax.dot_general(a, b, (((1,), (1,)), ((), ())), preferred_element_type=F32)


def _split(x):
    hi = x.astype(BF16)
    lo = (x - hi.astype(F32)).astype(BF16)
    return hi, lo


def _dot_f32(a, b):
    ah, al = _split(a)
    bh, bl = _split(b)
    return _dot(ah, bh) + (_dot(ah, bl) + _dot(al, bh))


def _silu(x):
    return x * jax.nn.sigmoid(x)


def _log_sigmoid(x):
    return jnp.minimum(x, 0.0) - jnp.log1p(jnp.exp(-jnp.abs(x)))


def _norm_mod(x, gain, scale, shift):
    y = x * lax.rsqrt(jnp.mean(x * x, axis=-1, keepdims=True) + EPS) * gain
    return y * (1.0 + scale) + shift


def _pick(n, pref):
    t = min(n, pref)
    while n % t:
        t //= 2
    return t


def _mod_spec(mod, l, k, grp, nargs):
    r, d = mod.shape[2], mod.shape[3] // 6
    if nargs == 1:
        return pl.BlockSpec((None, None, r, d), lambda i: (l, grp(i), 0, k))
    return pl.BlockSpec((None, None, r, d), lambda i, j: (l, grp(i), 0, k))


def _adaln_kernel(c_ref, w_ref, b_ref, o_ref):
    a = _silu(c_ref[...]).astype(BF16)
    o_ref[...] = _dot(a, w_ref[...].astype(BF16)) + b_ref[...]


def _adaln(c, w_ada, b_ada):
    depth, d, n = w_ada.shape
    r = c.shape[0]
    tn = _pick(n, 1024)
    return pl.pallas_call(
        _adaln_kernel,
        grid=(depth, n // tn),
        in_specs=[
            pl.BlockSpec((r, d), lambda l, j: (0, 0)),
            pl.BlockSpec((None, d, tn), lambda l, j: (l, 0, j)),
            pl.BlockSpec((None, 1, tn), lambda l, j: (l, 0, j)),
        ],
        out_specs=pl.BlockSpec((None, r, tn), lambda l, j: (l, 0, j)),
        out_shape=jax.ShapeDtypeStruct((depth, r, n), F32),
        compiler_params=_params("arbitrary", "arbitrary"),
        name="adaln",
    )(c, w_ada, b_ada.reshape(depth, 1, n))


def _win_kernel(x_ref, sh_ref, sc_ref, gain_ref, wa_ref, wb_ref, wfg_ref, proj_ref, fg_ref, h_scr, *, na):
    j = pl.program_id(1)

    @pl.when(j == 0)
    def _():
        h = _norm_mod(x_ref[...], gain_ref[...], sc_ref[...], sh_ref[...]).astype(BF16)
        h_scr[...] = h
        fg_ref[...] = _dot(h, wfg_ref[...])

    @pl.when(j < na)
    def _():
        proj_ref[...] = _dot(h_scr[...], wa_ref[...]).astype(proj_ref.dtype)

    @pl.when(j >= na)
    def _():
        proj_ref[...] = _dot(h_scr[...], wb_ref[...]).astype(proj_ref.dtype)


def _win(x, mod, l, gain, w_a, w_b, w_fg, rows_per_group, tm, out_dtype):
    m, d = x.shape
    n_a, n_b = w_a.shape[2], w_b.shape[2]
    n = n_a + n_b
    tn = _pick(int(np.gcd(n_a, n_b)), 1024)
    na = n_a // tn
    grp = lambda i: (i * tm) // rows_per_group
    return pl.pallas_call(
        functools.partial(_win_kernel, na=na),
        grid=(m // tm, n // tn),
        in_specs=[
            pl.BlockSpec((tm, d), lambda i, j: (i, 0)),
            _mod_spec(mod, l, 0, grp, 2),
            _mod_spec(mod, l, 1, grp, 2),
            pl.BlockSpec((None, 1, d), lambda i, j: (l, 0, 0)),
            pl.BlockSpec((None, d, tn), lambda i, j: (l, 0, jnp.minimum(j, na - 1))),
            pl.BlockSpec((None, d, tn), lambda i, j: (l, 0, jnp.maximum(j - na, 0))),
            pl.BlockSpec((None, d, LANES), lambda i, j: (l, 0, 0)),
        ],
        out_specs=[
            pl.BlockSpec((tm, tn), lambda i, j: (i, j)),
            pl.BlockSpec((tm, LANES), lambda i, j: (i, 0)),
        ],
        out_shape=[jax.ShapeDtypeStruct((m, n), out_dtype), jax.ShapeDtypeStruct((m, LANES), F32)],
        scratch_shapes=[pltpu.VMEM((tm, d), BF16)],
        compiler_params=_params("arbitrary", "arbitrary"),
        name="win",
    )(x, mod, mod, gain, w_a, w_b, w_fg)


GLA_VPU_LEVEL_MIN = 16


def _gla_tables(c):
    i = np.arange(c)[:, None]
    j = np.arange(c)[None, :]
    coefs = [(j <= i)]
    masks = [(i == j)]
    sizes = []
    b = c
    while b >= 2:
        mid = (i // b) * b + b // 2 - 1
        if b < GLA_VPU_LEVEL_MIN:
            coefs.append(((j > mid) & (j <= i)) | ((j > i) & (j <= mid)))
        masks.append((i // b == j // b) & (i % b >= b // 2) & (j % b < b // 2))
        sizes.append(b)
        b //= 2
    return np.concatenate(coefs, 0).astype(np.float32), np.stack(masks).astype(np.float32), tuple(sizes)


def _gla_chunk_kernel(q_ref, k_ref, v_ref, g_ref, fg_ref, wfg2_ref, bfg_ref, gain_ref, coef_ref, mask_ref,
                      o_ref, s_out_ref, s_scr, *, scale, chunk, sizes, heads):
    ci = pl.program_id(1)

    @pl.when(ci == 0)
    def _():
        s_scr[...] = jnp.zeros_like(s_scr)

    c = chunk
    dk, dv = s_scr.shape[1], s_scr.shape[2]
    log_a_all = _log_sigmoid(_dot_f32(fg_ref[...], wfg2_ref[...]) + bfg_ref[...]) * (1.0 / GATE_TEMP)
    coef = coef_ref[...]
    nt = min(c, LANES)
    for h in range(heads):
        log_a = log_a_all[:, h * dk:(h + 1) * dk]
        la_hi, la_mid = _split(log_a)
        la_lo = (log_a - la_hi.astype(F32) - la_mid.astype(F32)).astype(BF16)
        expo = _dot(coef, la_hi) + (_dot(coef, la_mid) + _dot(coef, la_lo))
        cum = expo[0:c]
        d_cum = jnp.exp(cum)
        d_tail = jnp.exp(cum[c - 1:c] - cum)

        q = q_ref[:, h * dk:(h + 1) * dk].astype(F32) * scale
        k = k_ref[:, h * dk:(h + 1) * dk].astype(F32)
        v = v_ref[:, h * dv:(h + 1) * dv].astype(BF16)

        scores = mask_ref[0] * _dot_nt(q.astype(BF16), k.astype(BF16))
        n_small = 0
        for lv, b in enumerate(sizes):
            if b >= GLA_VPU_LEVEL_MIN:
                blocks = cum.reshape(c // b, b, dk)
                d_lv = jnp.exp(-jnp.abs(blocks - blocks[:, b // 2 - 1:b // 2, :])).reshape(c, dk)
            else:
                n_small += 1
                d_lv = jnp.exp(expo[n_small * c:(n_small + 1) * c])
            scores = scores + mask_ref[1 + lv] * _dot_nt((q * d_lv).astype(BF16), (k * d_lv).astype(BF16))

        s = s_scr[h]
        o = _dot(scores.astype(BF16), v) + _dot((q * d_cum).astype(BF16), s.astype(BF16))

        a_col = jnp.transpose(d_cum[c - nt:c])[:, nt - 1:nt]
        k_tail_t = jnp.transpose(k * d_tail).astype(BF16)
        s_scr[h] = a_col * s + _dot(k_tail_t, v)

        o = o * lax.rsqrt(jnp.mean(o * o, axis=-1, keepdims=True) + EPS) * gain_ref[...]
        o_ref[:, h * dv:(h + 1) * dv] = (o * _silu(g_ref[:, h * dv:(h + 1) * dv].astype(F32))).astype(o_ref.dtype)

    @pl.when(ci == pl.num_programs(1) - 1)
    def _():
        s_out_ref[...] = s_scr[...]


def _gla_chunk(proj, fg, l, w_fg2, b_fg2, gain, batch, seq, heads, dk, dv, chunk):
    m = proj.shape[0]
    kw, vw = heads * dk, heads * dv
    nc = seq // chunk
    coef, mask, sizes = _gla_tables(chunk)
    rank_pad = w_fg2.shape[1]
    row = lambda b, c: b * nc + c
    kern = functools.partial(_gla_chunk_kernel, scale=float(dk) ** -0.5, chunk=chunk, sizes=sizes, heads=heads)
    return pl.pallas_call(
        kern,
        grid=(batch, nc),
        in_specs=[
            pl.BlockSpec((chunk, kw), lambda b, c: (row(b, c), 0)),
            pl.BlockSpec((chunk, kw), lambda b, c: (row(b, c), 1)),
            pl.BlockSpec((chunk, vw), lambda b, c: (row(b, c), (2 * kw) // vw)),
            pl.BlockSpec((chunk, vw), lambda b, c: (row(b, c), (2 * kw) // vw + 1)),
            pl.BlockSpec((chunk, LANES), lambda b, c: (row(b, c), 0)),
            pl.BlockSpec((None, rank_pad, kw), lambda b, c: (l, 0, 0)),
            pl.BlockSpec((None, 1, kw), lambda b, c: (l, 0, 0)),
            pl.BlockSpec((None, 1, dv), lambda b, c: (l, 0, 0)),
            pl.BlockSpec(coef.shape, lambda b, c: (0, 0)),
            pl.BlockSpec(mask.shape, lambda b, c: (0, 0, 0)),
        ],
        out_specs=[
            pl.BlockSpec((chunk, vw), lambda b, c: (row(b, c), 0)),
            pl.BlockSpec((None, heads, dk, dv), lambda b, c: (b, 0, 0, 0)),
        ],
        out_shape=[jax.ShapeDtypeStruct((m, vw), BF16), jax.ShapeDtypeStruct((batch, heads, dk, dv), F32)],
        scratch_shapes=[pltpu.VMEM((heads, dk, dv), F32)],
        compiler_params=_params("arbitrary", "arbitrary"),
        name="gla_chunk",
    )(proj, proj, proj, proj, fg, w_fg2, b_fg2, gain, jnp.asarray(coef, BF16), jnp.asarray(mask, F32))


def _gla_step_kernel(qt_ref, kt_ref, fgt_ref, wfg2t_ref, bfgt_ref, v_ref, g_ref, gain_ref, s_ref, carry_ref,
                     o_ref, s_out_ref, *, scale, bt, rank):
    del carry_ref
    w_t = wfg2t_ref[...]
    fg_t = fgt_ref[...]
    xg = bfgt_ref[...]
    for r in range(rank):
        xg = xg + w_t[:, r:r + 1] * fg_t[r:r + 1, :]
    a_t = jnp.exp(_log_sigmoid(xg) * (1.0 / GATE_TEMP))
    q_t = qt_ref[...] * scale
    k_t = kt_ref[...]
    for j in range(bt):
        s_new = a_t[:, j:j + 1] * s_ref[j] + k_t[:, j:j + 1] * v_ref[j:j + 1, :]
        s_out_ref[j] = s_new
        o = jnp.sum(q_t[:, j:j + 1] * s_new, axis=0, keepdims=True)
        o = o * lax.rsqrt(jnp.mean(o * o, axis=-1, keepdims=True) + EPS) * gain_ref[...]
        o_ref[j:j + 1, :] = o * _silu(g_ref[j:j + 1, :])


def _gla_step(proj, fg, state, new_state, l, w_fg2, b_fg2, gain, heads, dk, dv, rank, bt):
    bs = proj.shape[0]
    kw, vw = heads * dk, heads * dv
    nb = bs // bt
    rank_pad = w_fg2.shape[1]
    to_cols = lambda a: a.reshape(nb, bt, heads, dk).transpose(2, 0, 3, 1)
    q_t = to_cols(proj[:, :kw])
    k_t = to_cols(proj[:, kw:2 * kw])
    fg_t = fg.reshape(nb, bt, rank_pad).transpose(0, 2, 1)
    w_t = w_fg2[l].reshape(rank_pad, heads, dk).transpose(1, 2, 0)
    b_t = b_fg2[l].reshape(heads, dk, 1)
    kern = functools.partial(_gla_step_kernel, scale=float(dk) ** -0.5, bt=bt, rank=rank)
    state_spec = pl.BlockSpec((None, bt, None, dk, dv), lambda i, h: (l, i, h, 0, 0))
    in_specs = [
        pl.BlockSpec((None, None, dk, bt), lambda i, h: (h, i, 0, 0)),
        pl.BlockSpec((None, None, dk, bt), lambda i, h: (h, i, 0, 0)),
        pl.BlockSpec((None, rank_pad, bt), lambda i, h: (i, 0, 0)),
        pl.BlockSpec((None, dk, rank_pad), lambda i, h: (h, 0, 0)),
        pl.BlockSpec((None, dk, 1), lambda i, h: (h, 0, 0)),
        pl.BlockSpec((bt, dv), lambda i, h: (i, (2 * kw) // dv + h)),
        pl.BlockSpec((bt, dv), lambda i, h: (i, (2 * kw + vw) // dv + h)),
        pl.BlockSpec((None, 1, dv), lambda i, h: (l, 0, 0)),
        state_spec,
        pl.BlockSpec(memory_space=pl.ANY),
    ]
    args = [q_t, k_t, fg_t, w_t, b_t, proj, proj, gain, state, new_state]
    return pl.pallas_call(
        kern,
        grid=(nb, heads),
        in_specs=in_specs,
        out_specs=[pl.BlockSpec((bt, dv), lambda i, h: (i, h)), state_spec],
        out_shape=[jax.ShapeDtypeStruct((bs, vw), F32), jax.ShapeDtypeStruct(state.shape, F32)],
        input_output_aliases={len(args) - 1: 1},
        compiler_params=_params("arbitrary", "arbitrary"),
        name="gla_step",
    )(*args)


def _conv_seq_kernel(cb_ref, cc_ref, ch_ref, w_ref, o_ref, st_ref):
    u = cc_ref[...].astype(F32) * ch_ref[...].astype(F32)
    t = u.shape[0]
    rows = lax.broadcasted_iota(jnp.int32, u.shape, 0)
    u1 = jnp.where(rows >= 1, pltpu.roll(u, 1, 0), 0.0)
    u2 = jnp.where(rows >= 2, pltpu.roll(u, 2, 0), 0.0)
    w = w_ref[...]
    conv = w[0:1] * u2 + w[1:2] * u1 + w[2:3] * u
    o_ref[...] = (cb_ref[...].astype(F32) * conv).astype(o_ref.dtype)
    st_ref[...] = u[t - 2:t]


def _conv_seq(proj, l, conv_w, batch, seq, off, cw):
    assert conv_w.shape[1] == 3 and seq >= 2
    m = proj.shape[0]
    tw = _pick(cw, 512)
    nb = off // tw
    return pl.pallas_call(
        _conv_seq_kernel,
        grid=(batch, cw // tw),
        in_specs=[
            pl.BlockSpec((seq, tw), lambda b, j: (b, nb + j)),
            pl.BlockSpec((seq, tw), lambda b, j: (b, nb + cw // tw + j)),
            pl.BlockSpec((seq, tw), lambda b, j: (b, nb + 2 * (cw // tw) + j)),
            pl.BlockSpec((None, 3, tw), lambda b, j: (l, 0, j)),
        ],
        out_specs=[
            pl.BlockSpec((seq, tw), lambda b, j: (b, j)),
            pl.BlockSpec((None, 2, tw), lambda b, j: (b, 0, j)),
        ],
        out_shape=[jax.ShapeDtypeStruct((m, cw), BF16), jax.ShapeDtypeStruct((batch, 2, cw), F32)],
        compiler_params=_params("arbitrary", "arbitrary"),
        name="conv_seq",
    )(proj, proj, proj, conv_w)


def _conv_step_kernel(cb_ref, cc_ref, ch_ref, s0_ref, s1_ref, w_ref, o_ref, n0_ref, n1_ref):
    u = cc_ref[...] * ch_ref[...]
    w = w_ref[...]
    s1 = s1_ref[...]
    o_ref[...] = cb_ref[...] * (w[0:1] * s0_ref[...] + w[1:2] * s1 + w[2:3] * u)
    n0_ref[...] = s1
    n1_ref[...] = u


def _conv_step(proj, state, l, conv_w, off, cw):
    assert conv_w.shape[1] == 3 and state.shape[2] == 2
    bs = proj.shape[0]
    tw = _pick(cw, 512)
    nb, nw = off // tw, cw // tw
    st = state.reshape(state.shape[0], bs, 2 * cw)
    col = lambda k: pl.BlockSpec((bs, tw), lambda j: (0, k + j))
    stc = lambda k: pl.BlockSpec((None, bs, tw), lambda j: (l, 0, k + j))
    o, n0, n1 = pl.pallas_call(
        _conv_step_kernel,
        grid=(nw,),
        in_specs=[col(nb), col(nb + nw), col(nb + 2 * nw), stc(0), stc(nw),
                  pl.BlockSpec((None, 3, tw), lambda j: (l, 0, j))],
        out_specs=[col(0), col(0), col(0)],
        out_shape=[jax.ShapeDtypeStruct((bs, cw), F32)] * 3,
        compiler_params=_params("arbitrary"),
        name="conv_step",
    )(proj, proj, proj, st, st, conv_w)
    return o, jnp.stack([n0, n1], axis=1)


def _merge_kernel(oa_ref, ob_ref, ga_ref, gb_ref, wa_ref, wb_ref, y_ref):
    ya = _dot(oa_ref[...].astype(BF16), wa_ref[...])
    yb = _dot(ob_ref[...].astype(BF16), wb_ref[...])
    y = jax.nn.sigmoid(ga_ref[...].astype(F32)) * ya + jax.nn.sigmoid(gb_ref[...].astype(F32)) * yb
    y_ref[...] = y.astype(y_ref.dtype)


def _merge(oa, ob, proj, l, w_pa, w_pb, off_ga, tm):
    m, d = oa.shape[0], w_pa.shape[2]
    tn = _pick(d, 512)
    na, nbk = off_ga // tn, (off_ga + d) // tn
    return pl.pallas_call(
        _merge_kernel,
        grid=(m // tm, d // tn),
        in_specs=[
            pl.BlockSpec((tm, oa.shape[1]), lambda i, j: (i, 0)),
            pl.BlockSpec((tm, ob.shape[1]), lambda i, j: (i, 0)),
            pl.BlockSpec((tm, tn), lambda i, j: (i, na + j)),
            pl.BlockSpec((tm, tn), lambda i, j: (i, nbk + j)),
            pl.BlockSpec((None, w_pa.shape[1], tn), lambda i, j: (l, 0, j)),
            pl.BlockSpec((None, w_pb.shape[1], tn), lambda i, j: (l, 0, j)),
        ],
        out_specs=pl.BlockSpec((tm, tn), lambda i, j: (i, j)),
        out_shape=jax.ShapeDtypeStruct((m, d), BF16),
        compiler_params=_params("arbitrary", "arbitrary"),
        name="merge",
    )(oa, ob, proj, proj, w_pa, w_pb)


def _top2_route(logits, n_experts):
    lane = lax.broadcasted_iota(jnp.int32, logits.shape, 1).astype(F32)
    lg = jnp.where(lane < n_experts, logits, -jnp.inf)
    m1 = jnp.max(lg, axis=1, keepdims=True)
    i1 = jnp.min(jnp.where(lg == m1, lane, float(LANES)), axis=1, keepdims=True)
    lg2 = jnp.where(lane == i1, -jnp.inf, lg)
    m2 = jnp.max(lg2, axis=1, keepdims=True)
    i2 = jnp.min(jnp.where(lg2 == m2, lane, float(LANES)), axis=1, keepdims=True)
    e2 = jnp.exp(m2 - m1)
    den = 1.0 + e2
    return jnp.where(lane == 0.0, i1, jnp.where(lane == 1.0, i2, jnp.where(lane == 2.0, 1.0 / den, e2 / den)))


def _wo_kernel(y_ref, x_ref, g1_ref, sh_ref, sc_ref, gain_ref, w_ref, *rest, n_experts):
    if n_experts:
        router_ref, x1_ref, h_ref, route_ref = rest
    else:
        x1_ref, h_ref = rest
    x1 = x_ref[...] + g1_ref[...] * _dot(y_ref[...], w_ref[...])
    x1_ref[...] = x1
    h = _norm_mod(x1, gain_ref[...], sc_ref[...], sh_ref[...])
    h_ref[...] = h.astype(h_ref.dtype)
    if n_experts:
        route_ref[...] = _top2_route(_dot_f32(h, router_ref[...]), n_experts)


def _wo(y, x, mod, l, gain, w_o, router, lm, n_experts, rows_per_group, tm):
    m, d = x.shape
    grp = lambda i: (i * tm) // rows_per_group
    in_specs = [
        pl.BlockSpec((tm, d), lambda i: (i, 0)),
        pl.BlockSpec((tm, d), lambda i: (i, 0)),
        _mod_spec(mod, l, 2, grp, 1), _mod_spec(mod, l, 3, grp, 1), _mod_spec(mod, l, 4, grp, 1),
        pl.BlockSpec((None, 1, d), lambda i: (l, 0, 0)),
        pl.BlockSpec((None, d, d), lambda i: (l, 0, 0)),
    ]
    args = [y, x, mod, mod, mod, gain, w_o]
    out_specs = [pl.BlockSpec((tm, d), lambda i: (i, 0)), pl.BlockSpec((tm, d), lambda i: (i, 0))]
    out_shape = [jax.ShapeDtypeStruct((m, d), F32), jax.ShapeDtypeStruct((m, d), F32 if n_experts else BF16)]
    if n_experts:
        in_specs.append(pl.BlockSpec((None, d, LANES), lambda i: (lm, 0, 0)))
        args.append(router)
        out_specs.append(pl.BlockSpec((tm, LANES), lambda i: (i, 0)))
        out_shape.append(jax.ShapeDtypeStruct((m, LANES), F32))
    return pl.pallas_call(
        functools.partial(_wo_kernel, n_experts=n_experts),
        grid=(m // tm,),
        in_specs=in_specs,
        out_specs=out_specs,
        out_shape=out_shape,
        compiler_params=_params("arbitrary"),
        name="wo",
    )(*args)


def _ffn_kernel(h_ref, x_ref, g2_ref, w1_ref, w3_ref, w2_ref, o_ref, acc):
    f = pl.program_id(1)

    @pl.when(f == 0)
    def _():
        acc[...] = jnp.zeros_like(acc)

    h = h_ref[...]
    hid = _silu(_dot(h, w1_ref[...])) * _dot(h, w3_ref[...])
    acc[...] += _dot(hid.astype(BF16), w2_ref[...])

    @pl.when(f == pl.num_programs(1) - 1)
    def _():
        o_ref[...] = x_ref[...] + g2_ref[...] * acc[...]


def _ffn(h, x, mod, l, ld, w1, w3, w2, rows_per_group, tm):
    m, d = x.shape
    ff = w1.shape[2]
    tf = _pick(ff, 512)
    grp = lambda i: (i * tm) // rows_per_group
    return pl.pallas_call(
        _ffn_kernel,
        grid=(m // tm, ff // tf),
        in_specs=[
            pl.BlockSpec((tm, d), lambda i, f: (i, 0)),
            pl.BlockSpec((tm, d), lambda i, f: (i, 0)),
            _mod_spec(mod, l, 5, grp, 2),
            pl.BlockSpec((None, d, tf), lambda i, f: (ld, 0, f)),
            pl.BlockSpec((None, d, tf), lambda i, f: (ld, 0, f)),
            pl.BlockSpec((None, tf, d), lambda i, f: (ld, f, 0)),
        ],
        out_specs=pl.BlockSpec((tm, d), lambda i, f: (i, 0)),
        out_shape=jax.ShapeDtypeStruct((m, d), F32),
        scratch_shapes=[pltpu.VMEM((tm, d), F32)],
        compiler_params=_params("arbitrary", "arbitrary"),
        name="ffn",
    )(h, x, mod, w1, w3, w2)


def _route_tables(route, n_experts, tm):
    m = route.shape[0]
    p_rows = -(-(TOP_K * m + n_experts * (tm - 1)) // tm) * tm
    e_flat = jnp.concatenate([route[:, 0], route[:, 1]]).astype(I32)
    onehot = (e_flat[:, None] == jnp.arange(n_experts, dtype=I32)[None, :]).astype(I32)
    rank = jnp.sum((jnp.cumsum(onehot, axis=0) - 1) * onehot, axis=1)
    counts = jnp.sum(onehot, axis=0)
    padded = ((counts + tm - 1) // tm) * tm
    ends = jnp.cumsum(padded)
    dest = (jnp.sum(onehot * (ends - padded)[None, :], axis=1) + rank).astype(I32)
    tile_start = jnp.arange(p_rows // tm, dtype=I32) * tm
    tile_expert = jnp.minimum(jnp.sum((tile_start[:, None] >= ends[None, :]).astype(I32), axis=1), n_experts - 1)
    n_used = (ends[-1] // tm).astype(I32).reshape(1)
    src = jnp.zeros((p_rows,), I32).at[dest].set(jnp.tile(jnp.arange(m, dtype=I32), TOP_K))
    return src, tile_expert.astype(I32), n_used, dest


def _row_copy(table_ref, row, dst_ref, r, sem):
    return pltpu.make_async_copy(table_ref.at[pl.ds(row, 1), :], dst_ref.at[pl.ds(r, 1), :], sem)


def _gather_rows(idx_ref, base, n, table_ref, dst_ref, sem):
    def start(r, carry):
        _row_copy(table_ref, idx_ref[base + r], dst_ref, r, sem).start()
        return carry

    def wait(r, carry):
        _row_copy(table_ref, 0, dst_ref, r, sem).wait()
        return carry

    lax.fori_loop(0, n, start, 0, unroll=8)
    lax.fori_loop(0, n, wait, 0, unroll=8)


def _dispatch_kernel(src_ref, nu_ref, h_hbm, o_ref, sem):
    t = pl.program_id(0)
    tm = o_ref.shape[0]

    @pl.when(t < nu_ref[0])
    def _():
        _gather_rows(src_ref, t * tm, tm, h_hbm, o_ref, sem)

    @pl.when(t >= nu_ref[0])
    def _():
        o_ref[...] = jnp.zeros_like(o_ref)


def _dispatch(h, src, n_used, tm):
    d = h.shape[1]
    p_rows = src.shape[0]
    return pl.pallas_call(
        _dispatch_kernel,
        grid_spec=pltpu.PrefetchScalarGridSpec(
            num_scalar_prefetch=2,
            grid=(p_rows // tm,),
            in_specs=[pl.BlockSpec(memory_space=pl.ANY)],
            out_specs=pl.BlockSpec((tm, d), lambda t, src, nu: (t, 0)),
            scratch_shapes=[pltpu.SemaphoreType.DMA(())],
        ),
        out_shape=jax.ShapeDtypeStruct((p_rows, d), h.dtype),
        compiler_params=_params("arbitrary"),
        name="moe_dispatch",
    )(src, n_used, h)


def _ffn_grouped_kernel(te_ref, nu_ref, x_ref, w1_ref, w3_ref, w2_ref, o_ref, xb, acc):
    t, f = pl.program_id(0), pl.program_id(1)
    used = t < nu_ref[0]
    last = f == pl.num_programs(1) - 1

    @pl.when(used)
    def _():
        @pl.when(f == 0)
        def _():
            xb[...] = x_ref[...].astype(BF16)
            acc[...] = jnp.zeros_like(acc)

        x = xb[...]
        hid = _silu(_dot(x, w1_ref[...])) * _dot(x, w3_ref[...])
        acc[...] += _dot(hid.astype(BF16), w2_ref[...])

    @pl.when(used & last)
    def _():
        o_ref[...] = acc[...]

    @pl.when(jnp.logical_not(used) & last)
    def _():
        o_ref[...] = jnp.zeros_like(o_ref)


def _ffn_grouped(xs, tile_expert, n_used, lm, w1, w3, w2, tm):
    p_rows, d = xs.shape
    ff = w1.shape[3]
    tf = _pick(ff, 512)
    nf = ff // tf
    fidx = lambda t, f, nu: jnp.where(t < nu[0], f, nf - 1)
    return pl.pallas_call(
        _ffn_grouped_kernel,
        grid_spec=pltpu.PrefetchScalarGridSpec(
            num_scalar_prefetch=2,
            grid=(p_rows // tm, nf),
            in_specs=[
                pl.BlockSpec((tm, d), lambda t, f, te, nu: (t, 0)),
                pl.BlockSpec((None, None, d, tf), lambda t, f, te, nu: (lm, te[t], 0, fidx(t, f, nu))),
                pl.BlockSpec((None, None, d, tf), lambda t, f, te, nu: (lm, te[t], 0, fidx(t, f, nu))),
                pl.BlockSpec((None, None, tf, d), lambda t, f, te, nu: (lm, te[t], fidx(t, f, nu), 0)),
            ],
            out_specs=pl.BlockSpec((tm, d), lambda t, f, te, nu: (t, 0)),
            scratch_shapes=[pltpu.VMEM((tm, d), BF16), pltpu.VMEM((tm, d), F32)],
        ),
        out_shape=jax.ShapeDtypeStruct((p_rows, d), F32),
        compiler_params=_params("arbitrary", "arbitrary"),
        name="moe_ffn",
    )(tile_expert, n_used, xs, w1, w3, w2)


def _combine_kernel(dest_ref, x_ref, g2_ref, route_ref, ys_hbm, o_ref, ybuf, sem, *, row0, m_all):
    tm = x_ref.shape[0]
    base = row0 + pl.program_id(0) * tm
    for slot in range(TOP_K):
        _gather_rows(dest_ref, slot * m_all + base, tm, ys_hbm, ybuf.at[slot], sem)
    route = route_ref[...]
    f = route[:, 2:3] * ybuf[0] + route[:, 3:4] * ybuf[1]
    o_ref[...] = x_ref[...] + g2_ref[...] * f


def _combine(x1, mod, l, route, ys, dest, row0, m_all, rows_per_group, tm):
    m, d = x1.shape
    grp = lambda i: (i * tm) // rows_per_group
    r = mod.shape[2]
    return pl.pallas_call(
        functools.partial(_combine_kernel, row0=row0, m_all=m_all),
        grid_spec=pltpu.PrefetchScalarGridSpec(
            num_scalar_prefetch=1,
            grid=(m // tm,),
            in_specs=[
                pl.BlockSpec((tm, d), lambda i, dest: (i, 0)),
                pl.BlockSpec((None, None, r, d), lambda i, dest: (l, grp(i), 0, 5)),
                pl.BlockSpec((tm, LANES), lambda i, dest: (i, 0)),
                pl.BlockSpec(memory_space=pl.ANY),
            ],
            out_specs=pl.BlockSpec((tm, d), lambda i, dest: (i, 0)),
            scratch_shapes=[pltpu.VMEM((TOP_K, tm, d), F32), pltpu.SemaphoreType.DMA(())],
        ),
        out_shape=jax.ShapeDtypeStruct((m, d), F32),
        compiler_params=_params("arbitrary"),
        name="moe_combine",
    )(dest, x1, mod, route, ys)


def _final_norm_kernel(x_ref, gain_ref, o_ref):
    x = x_ref[...]
    o_ref[...] = x * lax.rsqrt(jnp.mean(x * x, axis=-1, keepdims=True) + EPS) * gain_ref[...]


def _final_norm(x, gain, tm):
    m, d = x.shape
    return pl.pallas_call(
        _final_norm_kernel,
        grid=(m // tm,),
        in_specs=[pl.BlockSpec((tm, d), lambda i: (i, 0)), pl.BlockSpec((1, d), lambda i: (0, 0))],
        out_specs=pl.BlockSpec((tm, d), lambda i: (i, 0)),
        out_shape=jax.ShapeDtypeStruct((m, d), F32),
        compiler_params=_params("arbitrary"),
        name="final_norm",
    )(x, gain)


def kernel(x_prompt, x_sample, state_gla, state_conv, c_prompt, c_sample, w_ada, b_ada, norm1, norm2, w_in, w_fg2,
           b_fg2, gla_gain, conv_w, w_pa, w_pb, w_o, dense_w1, dense_w3, dense_w2, router, moe_w1, moe_w3, moe_w2,
           final_norm):
    depth, d = norm1.shape
    nb_p, seq, _ = x_prompt.shape
    nb_s = x_sample.shape[0]
    assert x_sample.shape[1] == 1
    _, _, heads, dk, dv = state_gla.shape
    kw, vw = heads * dk, heads * dv
    rank = w_fg2.shape[1]
    cw = conv_w.shape[-1]
    n_experts = router.shape[-1]
    assert rank <= LANES and n_experts <= LANES and TOP_K == 2
    m_p, m_s = nb_p * seq, nb_s
    off_conv = 2 * kw + 2 * vw
    off_ga = off_conv + 3 * cw

    w_a = w_in[:, :, :off_conv].astype(BF16)
    w_b = w_in[:, :, off_conv + rank:].astype(BF16)
    w_fg = jnp.pad(w_in[:, :, off_conv:off_conv + rank], ((0, 0), (0, 0), (0, LANES - rank))).astype(BF16)
    w_fg2p = jnp.pad(w_fg2, ((0, 0), (0, LANES - rank), (0, 0)))
    b_fg2r = b_fg2.reshape(depth, 1, kw)
    gain_r = gla_gain.reshape(depth, 1, dv)
    norm1r, norm2r = norm1.reshape(depth, 1, d), norm2.reshape(depth, 1, d)
    w_pab, w_pbb, w_ob = w_pa.astype(BF16), w_pb.astype(BF16), w_o.astype(BF16)
    d_w1, d_w3, d_w2 = dense_w1.astype(BF16), dense_w3.astype(BF16), dense_w2.astype(BF16)
    m_w1, m_w3, m_w2 = moe_w1.astype(BF16), moe_w3.astype(BF16), moe_w2.astype(BF16)
    router_p = jnp.pad(router, ((0, 0), (0, 0), (0, LANES - n_experts)))

    c_all = jnp.concatenate([c_prompt, c_sample], axis=0)
    mod = _adaln(jnp.pad(c_all, ((0, (-c_all.shape[0]) % 16), (0, 0))), w_ada, b_ada)
    mod_p = mod[:, :nb_p].reshape(depth, nb_p, 1, 6 * d)
    mod_s = mod[:, nb_p:nb_p + nb_s].reshape(depth, 1, nb_s, 6 * d)

    tm_p = _pick(seq, 512)
    tm_wo = _pick(seq, 256)
    xp, xs = x_prompt.reshape(m_p, d), x_sample.reshape(m_s, d)
    gla_p, conv_p, conv_s = [], [], []
    gla_s = jnp.zeros(state_gla.shape, F32)
    for l in range(depth):
        proj_p, fg_p = _win(xp, mod_p, l, norm1r, w_a, w_b, w_fg, seq, tm_p, BF16)
        proj_s, fg_s = _win(xs, mod_s, l, norm1r, w_a, w_b, w_fg, m_s, m_s, F32)
        oa_p, sg = _gla_chunk(proj_p, fg_p, l, w_fg2p, b_fg2r, gain_r, nb_p, seq, heads, dk, dv, _pick(seq, 256))
        gla_p.append(sg)
        oa_s, gla_s = _gla_step(proj_s, fg_s, state_gla, gla_s, l, w_fg2p, b_fg2r, gain_r, heads, dk, dv, rank,
                                  bt=_pick(m_s, 16))
        ob_p, sc = _conv_seq(proj_p, l, conv_w, nb_p, seq, off_conv, cw)
        conv_p.append(sc)
        ob_s, sc = _conv_step(proj_s, state_conv, l, conv_w, off_conv, cw)
        conv_s.append(sc)
        y_p = _merge(oa_p, ob_p, proj_p, l, w_pab, w_pbb, off_ga, tm_p)
        y_s = _merge(oa_s, ob_s, proj_s, l, w_pab, w_pbb, off_ga, m_s)
        if l % 2 == 0:
            x1_p, h_p = _wo(y_p, xp, mod_p, l, norm2r, w_ob, None, 0, 0, seq, tm_wo)
            x1_s, h_s = _wo(y_s, xs, mod_s, l, norm2r, w_ob, None, 0, 0, m_s, m_s)
            xp = _ffn(h_p, x1_p, mod_p, l, l // 2, d_w1, d_w3, d_w2, seq, tm_p)
            xs = _ffn(h_s, x1_s, mod_s, l, l // 2, d_w1, d_w3, d_w2, m_s, m_s)
        else:
            lm = l // 2
            x1_p, h_p, route_p = _wo(y_p, xp, mod_p, l, norm2r, w_ob, router_p, lm, n_experts, seq, tm_wo)
            x1_s, h_s, route_s = _wo(y_s, xs, mod_s, l, norm2r, w_ob, router_p, lm, n_experts, m_s, m_s)
            h_all = jnp.concatenate([h_p, h_s], axis=0)
            src, tile_expert, n_used, dest = _route_tables(jnp.concatenate([route_p, route_s], axis=0),
                                                           n_experts, MOE_TILE)
            ys = _ffn_grouped(_dispatch(h_all, src, n_used, MOE_TILE), tile_expert, n_used, lm, m_w1, m_w3, m_w2,
                              MOE_TILE)
            xp = _combine(x1_p, mod_p, l, route_p, ys, dest, 0, m_p + m_s, seq, tm_wo)
            xs = _combine(x1_s, mod_s, l, route_s, ys, dest, m_p, m_p + m_s, m_s, m_s)
    y_p = _final_norm(xp, final_norm.reshape(1, d), tm_p).reshape(nb_p, seq, d)
    y_s = _final_norm(xs, final_norm.reshape(1, d), m_s).reshape(nb_s, 1, d)
    return (y_p, y_s, jnp.stack(gla_p), jnp.stack(conv_p), gla_s, jnp.stack(conv_s))
```

```python
import functools

import numpy as np
import jax
import jax.numpy as jnp
from jax import lax
from jax.experimental import pallas as pl
from jax.experimental.pallas import tpu as pltpu

F32 = jnp.float32
BF16 = jnp.bfloat16
I32 = jnp.int32
EPS = 1e-6
GATE_TEMP = 16.0
TOP_K = 2
LANES = 128
VMEM_LIMIT_BYTES = 56 * 1024 * 1024
MOE_TILE = 512


def _params(*sem):
    return pltpu.CompilerParams(dimension_semantics=sem, vmem_limit_bytes=VMEM_LIMIT_BYTES)


def _dot(a, b):
    return jnp.dot(a, b, preferred_element_type=F32)


def _dot_nt(a, b):
    return lax.dot_general(a, b, (((1,), (1,)), ((), ())), preferred_element_type=F32)


def _split(x):
    hi = x.astype(BF16)
    lo = (x - hi.astype(F32)).astype(BF16)
    return hi, lo


def _dot_f32(a, b):
    ah, al = _split(a)
    bh, bl = _split(b)
    return _dot(ah, bh) + (_dot(ah, bl) + _dot(al, bh))


def _silu(x):
    return x * jax.nn.sigmoid(x)


def _log_sigmoid(x):
    return jnp.minimum(x, 0.0) - jnp.log1p(jnp.exp(-jnp.abs(x)))


def _norm_mod(x, gain, scale, shift):
    y = x * lax.rsqrt(jnp.mean(x * x, axis=-1, keepdims=True) + EPS) * gain
    return y * (1.0 + scale) + shift


def _pick(n, pref):
    t = min(n, pref)
    while n % t:
        t //= 2
    return t


def _mod_spec(mod, l, k, grp, nargs):
    r, d = mod.shape[2], mod.shape[3] // 6
    if nargs == 1:
        return pl.BlockSpec((None, None, r, d), lambda i: (l, grp(i), 0, k))
    return pl.BlockSpec((None, None, r, d), lambda i, j: (l, grp(i), 0, k))


def _adaln_kernel(c_ref, w_ref, b_ref, o_ref):
    a = _silu(c_ref[...]).astype(BF16)
    o_ref[...] = _dot(a, w_ref[...].astype(BF16)) + b_ref[...]


def _adaln(c, w_ada, b_ada):
    depth, d, n = w_ada.shape
    r = c.shape[0]
    tn = _pick(n, 1024)
    return pl.pallas_call(
        _adaln_kernel,
        grid=(depth, n // tn),
        in_specs=[
            pl.BlockSpec((r, d), lambda l, j: (0, 0)),
            pl.BlockSpec((None, d, tn), lambda l, j: (l, 0, j)),
            pl.BlockSpec((None, 1, tn), lambda l, j: (l, 0, j)),
        ],
        out_specs=pl.BlockSpec((None, r, tn), lambda l, j: (l, 0, j)),
        out_shape=jax.ShapeDtypeStruct((depth, r, n), F32),
        compiler_params=_params("arbitrary", "arbitrary"),
        name="adaln",
    )(c, w_ada, b_ada.reshape(depth, 1, n))


def _stage_w_in_kernel(w_ref, main_ref, fg_ref, *, off, rank):
    w = w_ref[...]
    main_ref[:, :off] = w[:, :off].astype(BF16)
    main_ref[:, off:] = w[:, off + rank:].astype(BF16)
    fg_ref[...] = jnp.zeros_like(fg_ref)
    fg_ref[:, :rank] = w[:, off:off + rank].astype(BF16)


def _stage_w_in(w_in, off, rank):
    depth, d, n = w_in.shape
    tr = _pick(d, 64)
    return pl.pallas_call(
        functools.partial(_stage_w_in_kernel, off=off, rank=rank),
        grid=(depth, d // tr),
        in_specs=[pl.BlockSpec((None, tr, n), lambda l, i: (l, i, 0))],
        out_specs=[
            pl.BlockSpec((None, tr, n - rank), lambda l, i: (l, i, 0)),
            pl.BlockSpec((None, tr, LANES), lambda l, i: (l, i, 0)),
        ],
        out_shape=[jax.ShapeDtypeStruct((depth, d, n - rank), BF16), jax.ShapeDtypeStruct((depth, d, LANES), BF16)],
        compiler_params=_params("arbitrary", "arbitrary"),
        name="stage_w_in",
    )(w_in)


def _win_kernel(x_ref, sh_ref, sc_ref, gain_ref, w_ref, wfg_ref, proj_ref, fg_ref, h_scr):
    @pl.when(pl.program_id(1) == 0)
    def _():
        h = _norm_mod(x_ref[...], gain_ref[...], sc_ref[...], sh_ref[...]).astype(BF16)
        h_scr[...] = h
        fg_ref[...] = _dot(h, wfg_ref[...])

    proj_ref[...] = _dot(h_scr[...], w_ref[...]).astype(proj_ref.dtype)


def _win(x, mod, l, gain, w_main, w_fg, rows_per_group, tm, out_dtype):
    m, d = x.shape
    n = w_main.shape[2]
    tn = _pick(n, 2048)
    grp = lambda i: (i * tm) // rows_per_group
    return pl.pallas_call(
        _win_kernel,
        grid=(m // tm, n // tn),
        in_specs=[
            pl.BlockSpec((tm, d), lambda i, j: (i, 0)),
            _mod_spec(mod, l, 0, grp, 2),
            _mod_spec(mod, l, 1, grp, 2),
            pl.BlockSpec((None, 1, d), lambda i, j: (l, 0, 0)),
            pl.BlockSpec((None, d, tn), lambda i, j: (l, 0, j)),
            pl.BlockSpec((None, d, LANES), lambda i, j: (l, 0, 0)),
        ],
        out_specs=[
            pl.BlockSpec((tm, tn), lambda i, j: (i, j)),
            pl.BlockSpec((tm, LANES), lambda i, j: (i, 0)),
        ],
        out_shape=[jax.ShapeDtypeStruct((m, n), out_dtype), jax.ShapeDtypeStruct((m, LANES), F32)],
        scratch_shapes=[pltpu.VMEM((tm, d), BF16)],
        compiler_params=_params("arbitrary", "arbitrary"),
        name="win",
    )(x, mod, mod, gain, w_main, w_fg)


GLA_VPU_LEVEL_MIN = 16


def _gla_tables(c):
    i = np.arange(c)[:, None]
    j = np.arange(c)[None, :]
    coefs = [(j <= i)]
    masks = [(i == j)]
    sizes = []
    b = c
    while b >= 2:
        mid = (i // b) * b + b // 2 - 1
        if b < GLA_VPU_LEVEL_MIN:
            coefs.append(((j > mid) & (j <= i)) | ((j > i) & (j <= mid)))
        masks.append((i // b == j // b) & (i % b >= b // 2) & (j % b < b // 2))
        sizes.append(b)
        b //= 2
    return np.concatenate(coefs, 0).astype(np.float32), np.stack(masks).astype(np.float32), tuple(sizes)


def _gla_chunk_kernel(q_ref, k_ref, v_ref, g_ref, fg_ref, wfg2_ref, bfg_ref, gain_ref, coef_ref, mask_ref,
                      o_ref, s_out_ref, s_scr, *, scale, chunk, sizes, heads):
    ci = pl.program_id(1)

    @pl.when(ci == 0)
    def _():
        s_scr[...] = jnp.zeros_like(s_scr)

    c = chunk
    dk, dv = s_scr.shape[1], s_scr.shape[2]
    log_a_all = _log_sigmoid(_dot_f32(fg_ref[...], wfg2_ref[...]) + bfg_ref[...]) * (1.0 / GATE_TEMP)
    coef = coef_ref[...]
    nt = min(c, LANES)
    for h in range(heads):
        log_a = log_a_all[:, h * dk:(h + 1) * dk]
        la_hi, la_mid = _split(log_a)
        la_lo = (log_a - la_hi.astype(F32) - la_mid.astype(F32)).astype(BF16)
        expo = _dot(coef, la_hi) + (_dot(coef, la_mid) + _dot(coef, la_lo))
        cum = expo[0:c]
        d_cum = jnp.exp(cum)
        d_tail = jnp.exp(cum[c - 1:c] - cum)

        q = q_ref[:, h * dk:(h + 1) * dk].astype(F32) * scale
        k = k_ref[:, h * dk:(h + 1) * dk].astype(F32)
        v = v_ref[:, h * dv:(h + 1) * dv].astype(BF16)

        scores = mask_ref[0] * _dot_nt(q.astype(BF16), k.astype(BF16))
        n_small = 0
        for lv, b in enumerate(sizes):
            if b >= GLA_VPU_LEVEL_MIN:
                blocks = cum.reshape(c // b, b, dk)
                d_lv = jnp.exp(-jnp.abs(blocks - blocks[:, b // 2 - 1:b // 2, :])).reshape(c, dk)
            else:
                n_small += 1
                d_lv = jnp.exp(expo[n_small * c:(n_small + 1) * c])
            scores = scores + mask_ref[1 + lv] * _dot_nt((q * d_lv).astype(BF16), (k * d_lv).astype(BF16))

        s = s_scr[h]
        o = _dot(scores.astype(BF16), v) + _dot((q * d_cum).astype(BF16), s.astype(BF16))

        a_col = jnp.transpose(d_cum[c - nt:c])[:, nt - 1:nt]
        k_tail_t = jnp.transpose(k * d_tail).astype(BF16)
        s_scr[h] = a_col * s + _dot(k_tail_t, v)

        o = o * lax.rsqrt(jnp.mean(o * o, axis=-1, keepdims=True) + EPS) * gain_ref[...]
        o_ref[:, h * dv:(h + 1) * dv] = (o * _silu(g_ref[:, h * dv:(h + 1) * dv].astype(F32))).astype(o_ref.dtype)

    @pl.when(ci == pl.num_programs(1) - 1)
    def _():
        s_out_ref[...] = s_scr[...]


def _gla_chunk(proj, fg, l, w_fg2, b_fg2, gain, batch, seq, heads, dk, dv, chunk):
    m = proj.shape[0]
    kw, vw = heads * dk, heads * dv
    nc = seq // chunk
    coef, mask, sizes = _gla_tables(chunk)
    rank_pad = w_fg2.shape[1]
    row = lambda b, c: b * nc + c
    kern = functools.partial(_gla_chunk_kernel, scale=float(dk) ** -0.5, chunk=chunk, sizes=sizes, heads=heads)
    return pl.pallas_call(
        kern,
        grid=(batch, nc),
        in_specs=[
            pl.BlockSpec((chunk, kw), lambda b, c: (row(b, c), 0)),
            pl.BlockSpec((chunk, kw), lambda b, c: (row(b, c), 1)),
            pl.BlockSpec((chunk, vw), lambda b, c: (row(b, c), (2 * kw) // vw)),
            pl.BlockSpec((chunk, vw), lambda b, c: (row(b, c), (2 * kw) // vw + 1)),
            pl.BlockSpec((chunk, LANES), lambda b, c: (row(b, c), 0)),
            pl.BlockSpec((None, rank_pad, kw), lambda b, c: (l, 0, 0)),
            pl.BlockSpec((None, 1, kw), lambda b, c: (l, 0, 0)),
            pl.BlockSpec((None, 1, dv), lambda b, c: (l, 0, 0)),
            pl.BlockSpec(coef.shape, lambda b, c: (0, 0)),
            pl.BlockSpec(mask.shape, lambda b, c: (0, 0, 0)),
        ],
        out_specs=[
            pl.BlockSpec((chunk, vw), lambda b, c: (row(b, c), 0)),
            pl.BlockSpec((None, heads, dk, dv), lambda b, c: (b, 0, 0, 0)),
        ],
        out_shape=[jax.ShapeDtypeStruct((m, vw), BF16), jax.ShapeDtypeStruct((batch, heads, dk, dv), F32)],
        scratch_shapes=[pltpu.VMEM((heads, dk, dv), F32)],
        compiler_params=_params("arbitrary", "arbitrary"),
        name="gla_chunk",
    )(proj, proj, proj, proj, fg, w_fg2, b_fg2, gain, jnp.asarray(coef, BF16), jnp.asarray(mask, F32))


def _gla_step_kernel(qt_ref, kt_ref, fgt_ref, wfg2t_ref, bfgt_ref, v_ref, g_ref, gain_ref, s_ref, carry_ref,
                     o_ref, s_out_ref, *, scale, bt, rank):
    del carry_ref
    w_t = wfg2t_ref[...]
    fg_t = fgt_ref[...]
    xg = bfgt_ref[...]
    for r in range(rank):
        xg = xg + w_t[:, r:r + 1] * fg_t[r:r + 1, :]
    a_t = jnp.exp(_log_sigmoid(xg) * (1.0 / GATE_TEMP))
    q_t = qt_ref[...] * scale
    k_t = kt_ref[...]
    for j in range(bt):
        s_new = a_t[:, j:j + 1] * s_ref[j] + k_t[:, j:j + 1] * v_ref[j:j + 1, :]
        s_out_ref[j] = s_new
        o = jnp.sum(q_t[:, j:j + 1] * s_new, axis=0, keepdims=True)
        o = o * lax.rsqrt(jnp.mean(o * o, axis=-1, keepdims=True) + EPS) * gain_ref[...]
        o_ref[j:j + 1, :] = o * _silu(g_ref[j:j + 1, :])


def _gla_step(proj, fg, state, new_state, l, w_fg2, b_fg2, gain, heads, dk, dv, rank, bt):
    bs = proj.shape[0]
    kw, vw = heads * dk, heads * dv
    nb = bs // bt
    rank_pad = w_fg2.shape[1]
    to_cols = lambda a: a.reshape(nb, bt, heads, dk).transpose(2, 0, 3, 1)
    q_t = to_cols(proj[:, :kw])
    k_t = to_cols(proj[:, kw:2 * kw])
    fg_t = fg.reshape(nb, bt, rank_pad).transpose(0, 2, 1)
    w_t = w_fg2[l].reshape(rank_pad, heads, dk).transpose(1, 2, 0)
    b_t = b_fg2[l].reshape(heads, dk, 1)
    kern = functools.partial(_gla_step_kernel, scale=float(dk) ** -0.5, bt=bt, rank=rank)
    state_spec = pl.BlockSpec((None, bt, None, dk, dv), lambda i, h: (l, i, h, 0, 0))
    in_specs = [
        pl.BlockSpec((None, None, dk, bt), lambda i, h: (h, i, 0, 0)),
        pl.BlockSpec((None, None, dk, bt), lambda i, h: (h, i, 0, 0)),
        pl.BlockSpec((None, rank_pad, bt), lambda i, h: (i, 0, 0)),
        pl.BlockSpec((None, dk, rank_pad), lambda i, h: (h, 0, 0)),
        pl.BlockSpec((None, dk, 1), lambda i, h: (h, 0, 0)),
        pl.BlockSpec((bt, dv), lambda i, h: (i, (2 * kw) // dv + h)),
        pl.BlockSpec((bt, dv), lambda i, h: (i, (2 * kw + vw) // dv + h)),
        pl.BlockSpec((None, 1, dv), lambda i, h: (l, 0, 0)),
        state_spec,
        pl.BlockSpec(memory_space=pl.ANY),
    ]
    args = [q_t, k_t, fg_t, w_t, b_t, proj, proj, gain, state, new_state]
    return pl.pallas_call(
        kern,
        grid=(nb, heads),
        in_specs=in_specs,
        out_specs=[pl.BlockSpec((bt, dv), lambda i, h: (i, h)), state_spec],
        out_shape=[jax.ShapeDtypeStruct((bs, vw), F32), jax.ShapeDtypeStruct(state.shape, F32)],
        input_output_aliases={len(args) - 1: 1},
        compiler_params=_params("arbitrary", "arbitrary"),
        name="gla_step",
    )(*args)


def _conv_seq_kernel(cb_ref, cc_ref, ch_ref, w_ref, o_ref, st_ref):
    u = cc_ref[...].astype(F32) * ch_ref[...].astype(F32)
    t = u.shape[0]
    rows = lax.broadcasted_iota(jnp.int32, u.shape, 0)
    u1 = jnp.where(rows >= 1, pltpu.roll(u, 1, 0), 0.0)
    u2 = jnp.where(rows >= 2, pltpu.roll(u, 2, 0), 0.0)
    w = w_ref[...]
    conv = w[0:1] * u2 + w[1:2] * u1 + w[2:3] * u
    o_ref[...] = (cb_ref[...].astype(F32) * conv).astype(o_ref.dtype)
    st_ref[...] = u[t - 2:t]


def _conv_seq(proj, l, conv_w, batch, seq, off, cw):
    assert conv_w.shape[1] == 3 and seq >= 2
    m = proj.shape[0]
    tw = _pick(cw, 512)
    nb = off // tw
    return pl.pallas_call(
        _conv_seq_kernel,
        grid=(batch, cw // tw),
        in_specs=[
            pl.BlockSpec((seq, tw), lambda b, j: (b, nb + j)),
            pl.BlockSpec((seq, tw), lambda b, j: (b, nb + cw // tw + j)),
            pl.BlockSpec((seq, tw), lambda b, j: (b, nb + 2 * (cw // tw) + j)),
            pl.BlockSpec((None, 3, tw), lambda b, j: (l, 0, j)),
        ],
        out_specs=[
            pl.BlockSpec((seq, tw), lambda b, j: (b, j)),
            pl.BlockSpec((None, 2, tw), lambda b, j: (b, 0, j)),
        ],
        out_shape=[jax.ShapeDtypeStruct((m, cw), BF16), jax.ShapeDtypeStruct((batch, 2, cw), F32)],
        compiler_params=_params("arbitrary", "arbitrary"),
        name="conv_seq",
    )(proj, proj, proj, conv_w)


def _conv_step_kernel(cb_ref, cc_ref, ch_ref, s0_ref, s1_ref, w_ref, o_ref, n0_ref, n1_ref):
    u = cc_ref[...] * ch_ref[...]
    w = w_ref[...]
    s1 = s1_ref[...]
    o_ref[...] = cb_ref[...] * (w[0:1] * s0_ref[...] + w[1:2] * s1 + w[2:3] * u)
    n0_ref[...] = s1
    n1_ref[...] = u


def _conv_step(proj, state, l, conv_w, off, cw):
    assert conv_w.shape[1] == 3 and state.shape[2] == 2
    bs = proj.shape[0]
    tw = _pick(cw, 512)
    nb, nw = off // tw, cw // tw
    st = state.reshape(state.shape[0], bs, 2 * cw)
    col = lambda k: pl.BlockSpec((bs, tw), lambda j: (0, k + j))
    stc = lambda k: pl.BlockSpec((None, bs, tw), lambda j: (l, 0, k + j))
    o, n0, n1 = pl.pallas_call(
        _conv_step_kernel,
        grid=(nw,),
        in_specs=[col(nb), col(nb + nw), col(nb + 2 * nw), stc(0), stc(nw),
                  pl.BlockSpec((None, 3, tw), lambda j: (l, 0, j))],
        out_specs=[col(0), col(0), col(0)],
        out_shape=[jax.ShapeDtypeStruct((bs, cw), F32)] * 3,
        compiler_params=_params("arbitrary"),
        name="conv_step",
    )(proj, proj, proj, st, st, conv_w)
    return o, jnp.stack([n0, n1], axis=1)


def _merge_kernel(oa_ref, ob_ref, ga_ref, gb_ref, wa_ref, wb_ref, y_ref):
    ya = _dot(oa_ref[...].astype(BF16), wa_ref[...])
    yb = _dot(ob_ref[...].astype(BF16), wb_ref[...])
    y = jax.nn.sigmoid(ga_ref[...].astype(F32)) * ya + jax.nn.sigmoid(gb_ref[...].astype(F32)) * yb
    y_ref[...] = y.astype(y_ref.dtype)


def _merge(oa, ob, proj, l, w_pa, w_pb, off_ga, tm):
    m, d = oa.shape[0], w_pa.shape[2]
    tn = _pick(d, 1024)
    na, nbk = off_ga // tn, (off_ga + d) // tn
    return pl.pallas_call(
        _merge_kernel,
        grid=(m // tm, d // tn),
        in_specs=[
            pl.BlockSpec((tm, oa.shape[1]), lambda i, j: (i, 0)),
            pl.BlockSpec((tm, ob.shape[1]), lambda i, j: (i, 0)),
            pl.BlockSpec((tm, tn), lambda i, j: (i, na + j)),
            pl.BlockSpec((tm, tn), lambda i, j: (i, nbk + j)),
            pl.BlockSpec((None, w_pa.shape[1], tn), lambda i, j: (l, 0, j)),
            pl.BlockSpec((None, w_pb.shape[1], tn), lambda i, j: (l, 0, j)),
        ],
        out_specs=pl.BlockSpec((tm, tn), lambda i, j: (i, j)),
        out_shape=jax.ShapeDtypeStruct((m, d), BF16),
        compiler_params=_params("arbitrary", "arbitrary"),
        name="merge",
    )(oa, ob, proj, proj, w_pa, w_pb)


def _top2_route(logits, n_experts):
    lane = lax.broadcasted_iota(jnp.int32, logits.shape, 1).astype(F32)
    lg = jnp.where(lane < n_experts, logits, -jnp.inf)
    m1 = jnp.max(lg, axis=1, keepdims=True)
    i1 = jnp.min(jnp.where(lg == m1, lane, float(LANES)), axis=1, keepdims=True)
    lg2 = jnp.where(lane == i1, -jnp.inf, lg)
    m2 = jnp.max(lg2, axis=1, keepdims=True)
    i2 = jnp.min(jnp.where(lg2 == m2, lane, float(LANES)), axis=1, keepdims=True)
    e2 = jnp.exp(m2 - m1)
    den = 1.0 + e2
    return jnp.where(lane == 0.0, i1, jnp.where(lane == 1.0, i2, jnp.where(lane == 2.0, 1.0 / den, e2 / den)))


def _wo_kernel(y_ref, x_ref, g1_ref, sh_ref, sc_ref, gain_ref, w_ref, *rest, n_experts):
    if n_experts:
        router_ref, x1_ref, h_ref, route_ref = rest
    else:
        x1_ref, h_ref = rest
    x1 = x_ref[...] + g1_ref[...] * _dot(y_ref[...], w_ref[...])
    x1_ref[...] = x1
    h = _norm_mod(x1, gain_ref[...], sc_ref[...], sh_ref[...])
    h_ref[...] = h.astype(h_ref.dtype)
    if n_experts:
        route_ref[...] = _top2_route(_dot_f32(h, router_ref[...]), n_experts)


def _wo(y, x, mod, l, gain, w_o, router, lm, n_experts, rows_per_group, tm):
    m, d = x.shape
    grp = lambda i: (i * tm) // rows_per_group
    in_specs = [
        pl.BlockSpec((tm, d), lambda i: (i, 0)),
        pl.BlockSpec((tm, d), lambda i: (i, 0)),
        _mod_spec(mod, l, 2, grp, 1), _mod_spec(mod, l, 3, grp, 1), _mod_spec(mod, l, 4, grp, 1),
        pl.BlockSpec((None, 1, d), lambda i: (l, 0, 0)),
        pl.BlockSpec((None, d, d), lambda i: (l, 0, 0)),
    ]
    args = [y, x, mod, mod, mod, gain, w_o]
    out_specs = [pl.BlockSpec((tm, d), lambda i: (i, 0)), pl.BlockSpec((tm, d), lambda i: (i, 0))]
    out_shape = [jax.ShapeDtypeStruct((m, d), F32), jax.ShapeDtypeStruct((m, d), F32 if n_experts else BF16)]
    if n_experts:
        in_specs.append(pl.BlockSpec((None, d, LANES), lambda i: (lm, 0, 0)))
        args.append(router)
        out_specs.append(pl.BlockSpec((tm, LANES), lambda i: (i, 0)))
        out_shape.append(jax.ShapeDtypeStruct((m, LANES), F32))
    return pl.pallas_call(
        functools.partial(_wo_kernel, n_experts=n_experts),
        grid=(m // tm,),
        in_specs=in_specs,
        out_specs=out_specs,
        out_shape=out_shape,
        compiler_params=_params("arbitrary"),
        name="wo",
    )(*args)


def _ffn_kernel(h_ref, x_ref, g2_ref, w1_ref, w3_ref, w2_ref, o_ref, acc):
    f = pl.program_id(1)

    @pl.when(f == 0)
    def _():
        acc[...] = jnp.zeros_like(acc)

    h = h_ref[...]
    hid = _silu(_dot(h, w1_ref[...])) * _dot(h, w3_ref[...])
    acc[...] += _dot(hid.astype(BF16), w2_ref[...])

    @pl.when(f == pl.num_programs(1) - 1)
    def _():
        o_ref[...] = x_ref[...] + g2_ref[...] * acc[...]


def _ffn(h, x, mod, l, ld, w1, w3, w2, rows_per_group, tm):
    m, d = x.shape
    ff = w1.shape[2]
    tf = _pick(ff, 512)
    grp = lambda i: (i * tm) // rows_per_group
    return pl.pallas_call(
        _ffn_kernel,
        grid=(m // tm, ff // tf),
        in_specs=[
            pl.BlockSpec((tm, d), lambda i, f: (i, 0)),
            pl.BlockSpec((tm, d), lambda i, f: (i, 0)),
            _mod_spec(mod, l, 5, grp, 2),
            pl.BlockSpec((None, d, tf), lambda i, f: (ld, 0, f)),
            pl.BlockSpec((None, d, tf), lambda i, f: (ld, 0, f)),
            pl.BlockSpec((None, tf, d), lambda i, f: (ld, f, 0)),
        ],
        out_specs=pl.BlockSpec((tm, d), lambda i, f: (i, 0)),
        out_shape=jax.ShapeDtypeStruct((m, d), F32),
        scratch_shapes=[pltpu.VMEM((tm, d), F32)],
        compiler_params=_params("arbitrary", "arbitrary"),
        name="ffn",
    )(h, x, mod, w1, w3, w2)


def _route_tables(route, n_experts, tm):
    m = route.shape[0]
    p_rows = -(-(TOP_K * m + n_experts * (tm - 1)) // tm) * tm
    e_flat = jnp.concatenate([route[:, 0], route[:, 1]]).astype(I32)
    onehot = (e_flat[:, None] == jnp.arange(n_experts, dtype=I32)[None, :]).astype(I32)
    rank = jnp.sum((jnp.cumsum(onehot, axis=0) - 1) * onehot, axis=1)
    counts = jnp.sum(onehot, axis=0)
    padded = ((counts + tm - 1) // tm) * tm
    ends = jnp.cumsum(padded)
    dest = (jnp.sum(onehot * (ends - padded)[None, :], axis=1) + rank).astype(I32)
    tile_start = jnp.arange(p_rows // tm, dtype=I32) * tm
    tile_expert = jnp.minimum(jnp.sum((tile_start[:, None] >= ends[None, :]).astype(I32), axis=1), n_experts - 1)
    n_used = (ends[-1] // tm).astype(I32).reshape(1)
    src = jnp.zeros((p_rows,), I32).at[dest].set(jnp.tile(jnp.arange(m, dtype=I32), TOP_K))
    return src, tile_expert.astype(I32), n_used, dest


def _row_copy(table_ref, row, dst_ref, r, sem):
    return pltpu.make_async_copy(table_ref.at[pl.ds(row, 1), :], dst_ref.at[pl.ds(r, 1), :], sem)


def _gather_rows(idx_ref, base, n, table_ref, dst_ref, sem):
    def start(r, carry):
        _row_copy(table_ref, idx_ref[base + r], dst_ref, r, sem).start()
        return carry

    def wait(r, carry):
        _row_copy(table_ref, 0, dst_ref, r, sem).wait()
        return carry

    lax.fori_loop(0, n, start, 0, unroll=8)
    lax.fori_loop(0, n, wait, 0, unroll=8)


def _dispatch_kernel(src_ref, nu_ref, h_hbm, o_ref, sem):
    t = pl.program_id(0)
    tm = o_ref.shape[0]

    @pl.when(t < nu_ref[0])
    def _():
        _gather_rows(src_ref, t * tm, tm, h_hbm, o_ref, sem)

    @pl.when(t >= nu_ref[0])
    def _():
        o_ref[...] = jnp.zeros_like(o_ref)


def _dispatch(h, src, n_used, tm):
    d = h.shape[1]
    p_rows = src.shape[0]
    return pl.pallas_call(
        _dispatch_kernel,
        grid_spec=pltpu.PrefetchScalarGridSpec(
            num_scalar_prefetch=2,
            grid=(p_rows // tm,),
            in_specs=[pl.BlockSpec(memory_space=pl.ANY)],
            out_specs=pl.BlockSpec((tm, d), lambda t, src, nu: (t, 0)),
            scratch_shapes=[pltpu.SemaphoreType.DMA(())],
        ),
        out_shape=jax.ShapeDtypeStruct((p_rows, d), h.dtype),
        compiler_params=_params("arbitrary"),
        name="moe_dispatch",
    )(src, n_used, h)


def _ffn_grouped_kernel(te_ref, nu_ref, x_ref, w1_ref, w3_ref, w2_ref, o_ref, xb, acc):
    t, f = pl.program_id(0), pl.program_id(1)
    used = t < nu_ref[0]
    last = f == pl.num_programs(1) - 1

    @pl.when(used)
    def _():
        @pl.when(f == 0)
        def _():
            xb[...] = x_ref[...].astype(BF16)
            acc[...] = jnp.zeros_like(acc)

        x = xb[...]
        hid = _silu(_dot(x, w1_ref[...])) * _dot(x, w3_ref[...])
        acc[...] += _dot(hid.astype(BF16), w2_ref[...])

    @pl.when(used & last)
    def _():
        o_ref[...] = acc[...]

    @pl.when(jnp.logical_not(used) & last)
    def _():
        o_ref[...] = jnp.zeros_like(o_ref)


def _ffn_grouped(xs, tile_expert, n_used, lm, w1, w3, w2, tm):
    p_rows, d = xs.shape
    ff = w1.shape[3]
    tf = _pick(ff, 512)
    nf = ff // tf
    fidx = lambda t, f, nu: jnp.where(t < nu[0], f, nf - 1)
    return pl.pallas_call(
        _ffn_grouped_kernel,
        grid_spec=pltpu.PrefetchScalarGridSpec(
            num_scalar_prefetch=2,
            grid=(p_rows // tm, nf),
            in_specs=[
                pl.BlockSpec((tm, d), lambda t, f, te, nu: (t, 0)),
                pl.BlockSpec((None, None, d, tf), lambda t, f, te, nu: (lm, te[t], 0, fidx(t, f, nu))),
                pl.BlockSpec((None, None, d, tf), lambda t, f, te, nu: (lm, te[t], 0, fidx(t, f, nu))),
                pl.BlockSpec((None, None, tf, d), lambda t, f, te, nu: (lm, te[t], fidx(t, f, nu), 0)),
            ],
            out_specs=pl.BlockSpec((tm, d), lambda t, f, te, nu: (t, 0)),
            scratch_shapes=[pltpu.VMEM((tm, d), BF16), pltpu.VMEM((tm, d), F32)],
        ),
        out_shape=jax.ShapeDtypeStruct((p_rows, d), F32),
        compiler_params=_params("arbitrary", "arbitrary"),
        name="moe_ffn",
    )(tile_expert, n_used, xs, w1, w3, w2)


def _combine_kernel(dest_ref, x_ref, g2_ref, route_ref, ys_hbm, o_ref, ybuf, sem, *, row0, m_all):
    tm = x_ref.shape[0]
    base = row0 + pl.program_id(0) * tm
    for slot in range(TOP_K):
        _gather_rows(dest_ref, slot * m_all + base, tm, ys_hbm, ybuf.at[slot], sem)
    route = route_ref[...]
    f = route[:, 2:3] * ybuf[0] + route[:, 3:4] * ybuf[1]
    o_ref[...] = x_ref[...] + g2_ref[...] * f


def _combine(x1, mod, l, route, ys, dest, row0, m_all, rows_per_group, tm):
    m, d = x1.shape
    grp = lambda i: (i * tm) // rows_per_group
    r = mod.shape[2]
    return pl.pallas_call(
        functools.partial(_combine_kernel, row0=row0, m_all=m_all),
        grid_spec=pltpu.PrefetchScalarGridSpec(
            num_scalar_prefetch=1,
            grid=(m // tm,),
            in_specs=[
                pl.BlockSpec((tm, d), lambda i, dest: (i, 0)),
                pl.BlockSpec((None, None, r, d), lambda i, dest: (l, grp(i), 0, 5)),
                pl.BlockSpec((tm, LANES), lambda i, dest: (i, 0)),
                pl.BlockSpec(memory_space=pl.ANY),
            ],
            out_specs=pl.BlockSpec((tm, d), lambda i, dest: (i, 0)),
            scratch_shapes=[pltpu.VMEM((TOP_K, tm, d), F32), pltpu.SemaphoreType.DMA(())],
        ),
        out_shape=jax.ShapeDtypeStruct((m, d), F32),
        compiler_params=_params("arbitrary"),
        name="moe_combine",
    )(dest, x1, mod, route, ys)


def _final_norm_kernel(x_ref, gain_ref, o_ref):
    x = x_ref[...]
    o_ref[...] = x * lax.rsqrt(jnp.mean(x * x, axis=-1, keepdims=True) + EPS) * gain_ref[...]


def _final_norm(x, gain, tm):
    m, d = x.shape
    return pl.pallas_call(
        _final_norm_kernel,
        grid=(m // tm,),
        in_specs=[pl.BlockSpec((tm, d), lambda i: (i, 0)), pl.BlockSpec((1, d), lambda i: (0, 0))],
        out_specs=pl.BlockSpec((tm, d), lambda i: (i, 0)),
        out_shape=jax.ShapeDtypeStruct((m, d), F32),
        compiler_params=_params("arbitrary"),
        name="final_norm",
    )(x, gain)


def kernel(x_prompt, x_sample, state_gla, state_conv, c_prompt, c_sample, w_ada, b_ada, norm1, norm2, w_in, w_fg2,
           b_fg2, gla_gain, conv_w, w_pa, w_pb, w_o, dense_w1, dense_w3, dense_w2, router, moe_w1, moe_w3, moe_w2,
           final_norm):
    depth, d = norm1.shape
    nb_p, seq, _ = x_prompt.shape
    nb_s = x_sample.shape[0]
    assert x_sample.shape[1] == 1
    _, _, heads, dk, dv = state_gla.shape
    kw, vw = heads * dk, heads * dv
    rank = w_fg2.shape[1]
    cw = conv_w.shape[-1]
    n_experts = router.shape[-1]
    assert rank <= LANES and n_experts <= LANES and TOP_K == 2
    m_p, m_s = nb_p * seq, nb_s
    off_conv = 2 * kw + 2 * vw
    off_ga = off_conv + 3 * cw

    w_main, w_fg = _stage_w_in(w_in, off_conv, rank)
    w_fg2p = jnp.pad(w_fg2, ((0, 0), (0, LANES - rank), (0, 0)))
    b_fg2r = b_fg2.reshape(depth, 1, kw)
    gain_r = gla_gain.reshape(depth, 1, dv)
    norm1r, norm2r = norm1.reshape(depth, 1, d), norm2.reshape(depth, 1, d)
    w_pab, w_pbb, w_ob = w_pa.astype(BF16), w_pb.astype(BF16), w_o.astype(BF16)
    d_w1, d_w3, d_w2 = dense_w1.astype(BF16), dense_w3.astype(BF16), dense_w2.astype(BF16)
    m_w1, m_w3, m_w2 = moe_w1.astype(BF16), moe_w3.astype(BF16), moe_w2.astype(BF16)
    router_p = jnp.pad(router, ((0, 0), (0, 0), (0, LANES - n_experts)))

    c_all = jnp.concatenate([c_prompt, c_sample], axis=0)
    mod = _adaln(jnp.pad(c_all, ((0, (-c_all.shape[0]) % 16), (0, 0))), w_ada, b_ada)
    mod_p = mod[:, :nb_p].reshape(depth, nb_p, 1, 6 * d)
    mod_s = mod[:, nb_p:nb_p + nb_s].reshape(depth, 1, nb_s, 6 * d)

    tm_p = _pick(seq, 512)
    tm_wo = _pick(seq, 256)
    xp, xs = x_prompt.reshape(m_p, d), x_sample.reshape(m_s, d)
    gla_p, conv_p, conv_s = [], [], []
    gla_s = jnp.zeros(state_gla.shape, F32)
    for l in range(depth):
        proj_p, fg_p = _win(xp, mod_p, l, norm1r, w_main, w_fg, seq, tm_p, BF16)
        proj_s, fg_s = _win(xs, mod_s, l, norm1r, w_main, w_fg, m_s, m_s, F32)
        oa_p, sg = _gla_chunk(proj_p, fg_p, l, w_fg2p, b_fg2r, gain_r, nb_p, seq, heads, dk, dv, _pick(seq, 256))
        gla_p.append(sg)
        oa_s, gla_s = _gla_step(proj_s, fg_s, state_gla, gla_s, l, w_fg2p, b_fg2r, gain_r, heads, dk, dv, rank,
                                  bt=_pick(m_s, 16))
        ob_p, sc = _conv_seq(proj_p, l, conv_w, nb_p, seq, off_conv, cw)
        conv_p.append(sc)
        ob_s, sc = _conv_step(proj_s, state_conv, l, conv_w, off_conv, cw)
        conv_s.append(sc)
        y_p = _merge(oa_p, ob_p, proj_p, l, w_pab, w_pbb, off_ga, tm_p)
        y_s = _merge(oa_s, ob_s, proj_s, l, w_pab, w_pbb, off_ga, m_s)
        if l % 2 == 0:
            x1_p, h_p = _wo(y_p, xp, mod_p, l, norm2r, w_ob, None, 0, 0, seq, tm_wo)
            x1_s, h_s = _wo(y_s, xs, mod_s, l, norm2r, w_ob, None, 0, 0, m_s, m_s)
            xp = _ffn(h_p, x1_p, mod_p, l, l // 2, d_w1, d_w3, d_w2, seq, tm_p)
            xs = _ffn(h_s, x1_s, mod_s, l, l // 2, d_w1, d_w3, d_w2, m_s, m_s)
        else:
            lm = l // 2
            x1_p, h_p, route_p = _wo(y_p, xp, mod_p, l, norm2r, w_ob, router_p, lm, n_experts, seq, tm_wo)
            x1_s, h_s, route_s = _wo(y_s, xs, mod_s, l, norm2r, w_ob, router_p, lm, n_experts, m_s, m_s)
            h_all = jnp.concatenate([h_p, h_s], axis=0)
            src, tile_expert, n_used, dest = _route_tables(jnp.concatenate([route_p, route_s], axis=0),
                                                           n_experts, MOE_TILE)
            ys = _ffn_grouped(_dispatch(h_all, src, n_used, MOE_TILE), tile_expert, n_used, lm, m_w1, m_w3, m_w2,
                              MOE_TILE)
            xp = _combine(x1_p, mod_p, l, route_p, ys, dest, 0, m_p + m_s, seq, tm_wo)
            xs = _combine(x1_s, mod_s, l, route_s, ys, dest, m_p, m_p + m_s, m_s, m_s)
    y_p = _final_norm(xp, final_norm.reshape(1, d), tm_p).reshape(nb_p, seq, d)
    y_s = _final_norm(xs, final_norm.reshape(1, d), m_s).reshape(nb_s, 1, d)
    return (y_p, y_s, jnp.stack(gla_p), jnp.stack(conv_p), gla_s, jnp.stack(conv_s))
```

```python
import functools

import numpy as np
import jax
import jax.numpy as jnp
from jax import lax
from jax.experimental import pallas as pl
from jax.experimental.pallas import tpu as pltpu

F32 = jnp.float32
BF16 = jnp.bfloat16
I32 = jnp.int32
EPS = 1e-6
GATE_TEMP = 16.0
TOP_K = 2
LANES = 128
VMEM_LIMIT_BYTES = 56 * 1024 * 1024
MOE_TILE = 512


def _params(*sem):
    return pltpu.CompilerParams(dimension_semantics=sem, vmem_limit_bytes=VMEM_LIMIT_BYTES)


def _dot(a, b):
    return jnp.dot(a, b, preferred_element_type=F32)


def _dot_nt(a, b):
    return lax.dot_general(a, b, (((1,), (1,)), ((), ())), preferred_element_type=F32)


def _split(x):
    hi = x.astype(BF16)
    lo = (x - hi.astype(F32)).astype(BF16)
    return hi, lo


def _dot_f32(a, b):
    ah, al = _split(a)
    bh, bl = _split(b)
    return _dot(ah, bh) + (_dot(ah, bl) + _dot(al, bh))


def _silu(x):
    return x * jax.nn.sigmoid(x)


def _log_sigmoid(x):
    return jnp.minimum(x, 0.0) - jnp.log1p(jnp.exp(-jnp.abs(x)))


def _norm_mod(x, gain, scale, shift):
    y = x * lax.rsqrt(jnp.mean(x * x, axis=-1, keepdims=True) + EPS) * gain
    return y * (1.0 + scale) + shift


def _pick(n, pref):
    t = min(n, pref)
    while n % t:
        t //= 2
    return t


def _mod_spec(mod, l, k, grp, nargs):
    r, d = mod.shape[2], mod.shape[3] // 6
    if nargs == 1:
        return pl.BlockSpec((None, None, r, d), lambda i: (l, grp(i), 0, k))
    return pl.BlockSpec((None, None, r, d), lambda i, j: (l, grp(i), 0, k))


def _adaln_kernel(c_ref, w_ref, b_ref, o_ref):
    a = _silu(c_ref[...]).astype(BF16)
    o_ref[...] = _dot(a, w_ref[...].astype(BF16)) + b_ref[...]


def _adaln(c, w_ada, b_ada):
    depth, d, n = w_ada.shape
    r = c.shape[0]
    tn = _pick(n, 1024)
    return pl.pallas_call(
        _adaln_kernel,
        grid=(depth, n // tn),
        in_specs=[
            pl.BlockSpec((r, d), lambda l, j: (0, 0)),
            pl.BlockSpec((None, d, tn), lambda l, j: (l, 0, j)),
            pl.BlockSpec((None, 1, tn), lambda l, j: (l, 0, j)),
        ],
        out_specs=pl.BlockSpec((None, r, tn), lambda l, j: (l, 0, j)),
        out_shape=jax.ShapeDtypeStruct((depth, r, n), F32),
        compiler_params=_params("arbitrary", "arbitrary"),
        name="adaln",
    )(c, w_ada, b_ada.reshape(depth, 1, n))


def _stage_w_in_kernel(wt_ref, fgt_ref, main_ref, fg_ref, *, rank):
    main_ref[...] = jnp.transpose(wt_ref[0]).astype(BF16)

    @pl.when(pl.program_id(1) == 0)
    def _():
        fg = jnp.transpose(fgt_ref[...])
        lane = lax.broadcasted_iota(jnp.int32, fg.shape, 1)
        fg_ref[...] = jnp.where(lane < rank, fg, 0.0).astype(BF16)


def _stage_w_in(w_in, off, rank):
    depth, d, n = w_in.shape
    w_t = jnp.swapaxes(w_in, 1, 2)
    tc = _pick(int(np.gcd(off, n - rank - off)), 512)
    assert off % LANES == 0 and rank % 8 == 0
    src_row = lambda j: pl.multiple_of(j * tc + jnp.where(j * tc >= off, rank, 0), 8)
    return pl.pallas_call(
        functools.partial(_stage_w_in_kernel, rank=rank),
        grid=(depth, (n - rank) // tc),
        in_specs=[
            pl.BlockSpec((pl.Element(1), pl.Element(tc), pl.Element(d)), lambda l, j: (l, src_row(j), 0)),
            pl.BlockSpec((None, LANES, d), lambda l, j: (l, off // LANES, 0)),
        ],
        out_specs=[
            pl.BlockSpec((None, d, tc), lambda l, j: (l, 0, j)),
            pl.BlockSpec((None, d, LANES), lambda l, j: (l, 0, 0)),
        ],
        out_shape=[jax.ShapeDtypeStruct((depth, d, n - rank), BF16), jax.ShapeDtypeStruct((depth, d, LANES), BF16)],
        compiler_params=_params("arbitrary", "arbitrary"),
        name="stage_w_in",
    )(w_t, w_t)


def _win_kernel(x_ref, sh_ref, sc_ref, gain_ref, w_ref, wfg_ref, proj_ref, fg_ref, h_scr):
    @pl.when(pl.program_id(1) == 0)
    def _():
        h = _norm_mod(x_ref[...], gain_ref[...], sc_ref[...], sh_ref[...]).astype(BF16)
        h_scr[...] = h
        fg_ref[...] = _dot(h, wfg_ref[...])

    proj_ref[...] = _dot(h_scr[...], w_ref[...]).astype(proj_ref.dtype)


def _win(x, mod, l, gain, w_main, w_fg, rows_per_group, tm, out_dtype):
    m, d = x.shape
    n = w_main.shape[2]
    tn = _pick(n, 2048)
    grp = lambda i: (i * tm) // rows_per_group
    return pl.pallas_call(
        _win_kernel,
        grid=(m // tm, n // tn),
        in_specs=[
            pl.BlockSpec((tm, d), lambda i, j: (i, 0)),
            _mod_spec(mod, l, 0, grp, 2),
            _mod_spec(mod, l, 1, grp, 2),
            pl.BlockSpec((None, 1, d), lambda i, j: (l, 0, 0)),
            pl.BlockSpec((None, d, tn), lambda i, j: (l, 0, j)),
            pl.BlockSpec((None, d, LANES), lambda i, j: (l, 0, 0)),
        ],
        out_specs=[
            pl.BlockSpec((tm, tn), lambda i, j: (i, j)),
            pl.BlockSpec((tm, LANES), lambda i, j: (i, 0)),
        ],
        out_shape=[jax.ShapeDtypeStruct((m, n), out_dtype), jax.ShapeDtypeStruct((m, LANES), F32)],
        scratch_shapes=[pltpu.VMEM((tm, d), BF16)],
        compiler_params=_params("arbitrary", "arbitrary"),
        name="win",
    )(x, mod, mod, gain, w_main, w_fg)


GLA_VPU_LEVEL_MIN = 16


def _gla_tables(c):
    i = np.arange(c)[:, None]
    j = np.arange(c)[None, :]
    coefs = [(j <= i)]
    masks = [(i == j)]
    sizes = []
    b = c
    while b >= 2:
        mid = (i // b) * b + b // 2 - 1
        if b < GLA_VPU_LEVEL_MIN:
            coefs.append(((j > mid) & (j <= i)) | ((j > i) & (j <= mid)))
        masks.append((i // b == j // b) & (i % b >= b // 2) & (j % b < b // 2))
        sizes.append(b)
        b //= 2
    return np.concatenate(coefs, 0).astype(np.float32), np.stack(masks).astype(np.float32), tuple(sizes)


def _gla_chunk_kernel(q_ref, k_ref, v_ref, g_ref, fg_ref, wfg2_ref, bfg_ref, gain_ref, coef_ref, mask_ref,
                      o_ref, s_out_ref, s_scr, *, scale, chunk, sizes, heads):
    ci = pl.program_id(1)

    @pl.when(ci == 0)
    def _():
        s_scr[...] = jnp.zeros_like(s_scr)

    c = chunk
    dk, dv = s_scr.shape[1], s_scr.shape[2]
    log_a_all = _log_sigmoid(_dot_f32(fg_ref[...], wfg2_ref[...]) + bfg_ref[...]) * (1.0 / GATE_TEMP)
    coef = coef_ref[...]
    nt = min(c, LANES)
    for h in range(heads):
        log_a = log_a_all[:, h * dk:(h + 1) * dk]
        la_hi, la_mid = _split(log_a)
        la_lo = (log_a - la_hi.astype(F32) - la_mid.astype(F32)).astype(BF16)
        expo = _dot(coef, la_hi) + (_dot(coef, la_mid) + _dot(coef, la_lo))
        cum = expo[0:c]
        d_cum = jnp.exp(cum)
        d_tail = jnp.exp(cum[c - 1:c] - cum)

        q = q_ref[:, h * dk:(h + 1) * dk].astype(F32) * scale
        k = k_ref[:, h * dk:(h + 1) * dk].astype(F32)
        v = v_ref[:, h * dv:(h + 1) * dv].astype(BF16)

        scores = mask_ref[0] * _dot_nt(q.astype(BF16), k.astype(BF16))
        n_small = 0
        for lv, b in enumerate(sizes):
            if b >= GLA_VPU_LEVEL_MIN:
                blocks = cum.reshape(c // b, b, dk)
                d_lv = jnp.exp(-jnp.abs(blocks - blocks[:, b // 2 - 1:b // 2, :])).reshape(c, dk)
            else:
                n_small += 1
                d_lv = jnp.exp(expo[n_small * c:(n_small + 1) * c])
            scores = scores + mask_ref[1 + lv] * _dot_nt((q * d_lv).astype(BF16), (k * d_lv).astype(BF16))

        s = s_scr[h]
        o = _dot(scores.astype(BF16), v) + _dot((q * d_cum).astype(BF16), s.astype(BF16))

        a_col = jnp.transpose(d_cum[c - nt:c])[:, nt - 1:nt]
        k_tail_t = jnp.transpose(k * d_tail).astype(BF16)
        s_scr[h] = a_col * s + _dot(k_tail_t, v)

        o = o * lax.rsqrt(jnp.mean(o * o, axis=-1, keepdims=True) + EPS) * gain_ref[...]
        o_ref[:, h * dv:(h + 1) * dv] = (o * _silu(g_ref[:, h * dv:(h + 1) * dv].astype(F32))).astype(o_ref.dtype)

    @pl.when(ci == pl.num_programs(1) - 1)
    def _():
        s_out_ref[...] = s_scr[...]


def _gla_chunk(proj, fg, l, w_fg2, b_fg2, gain, batch, seq, heads, dk, dv, chunk):
    m = proj.shape[0]
    kw, vw = heads * dk, heads * dv
    nc = seq // chunk
    coef, mask, sizes = _gla_tables(chunk)
    rank_pad = w_fg2.shape[1]
    row = lambda b, c: b * nc + c
    kern = functools.partial(_gla_chunk_kernel, scale=float(dk) ** -0.5, chunk=chunk, sizes=sizes, heads=heads)
    return pl.pallas_call(
        kern,
        grid=(batch, nc),
        in_specs=[
            pl.BlockSpec((chunk, kw), lambda b, c: (row(b, c), 0)),
            pl.BlockSpec((chunk, kw), lambda b, c: (row(b, c), 1)),
            pl.BlockSpec((chunk, vw), lambda b, c: (row(b, c), (2 * kw) // vw)),
            pl.BlockSpec((chunk, vw), lambda b, c: (row(b, c), (2 * kw) // vw + 1)),
            pl.BlockSpec((chunk, LANES), lambda b, c: (row(b, c), 0)),
            pl.BlockSpec((None, rank_pad, kw), lambda b, c: (l, 0, 0)),
            pl.BlockSpec((None, 1, kw), lambda b, c: (l, 0, 0)),
            pl.BlockSpec((None, 1, dv), lambda b, c: (l, 0, 0)),
            pl.BlockSpec(coef.shape, lambda b, c: (0, 0)),
            pl.BlockSpec(mask.shape, lambda b, c: (0, 0, 0)),
        ],
        out_specs=[
            pl.BlockSpec((chunk, vw), lambda b, c: (row(b, c), 0)),
            pl.BlockSpec((None, heads, dk, dv), lambda b, c: (b, 0, 0, 0)),
        ],
        out_shape=[jax.ShapeDtypeStruct((m, vw), BF16), jax.ShapeDtypeStruct((batch, heads, dk, dv), F32)],
        scratch_shapes=[pltpu.VMEM((heads, dk, dv), F32)],
        compiler_params=_params("arbitrary", "arbitrary"),
        name="gla_chunk",
    )(proj, proj, proj, proj, fg, w_fg2, b_fg2, gain, jnp.asarray(coef, BF16), jnp.asarray(mask, F32))


def _gla_step_kernel(qt_ref, kt_ref, fgt_ref, wfg2t_ref, bfgt_ref, v_ref, g_ref, gain_ref, s_ref, carry_ref,
                     o_ref, s_out_ref, *, scale, bt, rank):
    del carry_ref
    w_t = wfg2t_ref[...]
    fg_t = fgt_ref[...]
    xg = bfgt_ref[...]
    for r in range(rank):
        xg = xg + w_t[:, r:r + 1] * fg_t[r:r + 1, :]
    a_t = jnp.exp(_log_sigmoid(xg) * (1.0 / GATE_TEMP))
    q_t = qt_ref[...] * scale
    k_t = kt_ref[...]
    for j in range(bt):
        s_new = a_t[:, j:j + 1] * s_ref[j] + k_t[:, j:j + 1] * v_ref[j:j + 1, :]
        s_out_ref[j] = s_new
        o = jnp.sum(q_t[:, j:j + 1] * s_new, axis=0, keepdims=True)
        o = o * lax.rsqrt(jnp.mean(o * o, axis=-1, keepdims=True) + EPS) * gain_ref[...]
        o_ref[j:j + 1, :] = o * _silu(g_ref[j:j + 1, :])


def _gla_step(proj, fg, state, new_state, l, w_fg2, b_fg2, gain, heads, dk, dv, rank, bt):
    bs = proj.shape[0]
    kw, vw = heads * dk, heads * dv
    nb = bs // bt
    rank_pad = w_fg2.shape[1]
    to_cols = lambda a: a.reshape(nb, bt, heads, dk).transpose(2, 0, 3, 1)
    q_t = to_cols(proj[:, :kw])
    k_t = to_cols(proj[:, kw:2 * kw])
    fg_t = fg.reshape(nb, bt, rank_pad).transpose(0, 2, 1)
    w_t = w_fg2[l].reshape(rank_pad, heads, dk).transpose(1, 2, 0)
    b_t = b_fg2[l].reshape(heads, dk, 1)
    kern = functools.partial(_gla_step_kernel, scale=float(dk) ** -0.5, bt=bt, rank=rank)
    state_spec = pl.BlockSpec((None, bt, None, dk, dv), lambda i, h: (l, i, h, 0, 0))
    in_specs = [
        pl.BlockSpec((None, None, dk, bt), lambda i, h: (h, i, 0, 0)),
        pl.BlockSpec((None, None, dk, bt), lambda i, h: (h, i, 0, 0)),
        pl.BlockSpec((None, rank_pad, bt), lambda i, h: (i, 0, 0)),
        pl.BlockSpec((None, dk, rank_pad), lambda i, h: (h, 0, 0)),
        pl.BlockSpec((None, dk, 1), lambda i, h: (h, 0, 0)),
        pl.BlockSpec((bt, dv), lambda i, h: (i, (2 * kw) // dv + h)),
        pl.BlockSpec((bt, dv), lambda i, h: (i, (2 * kw + vw) // dv + h)),
        pl.BlockSpec((None, 1, dv), lambda i, h: (l, 0, 0)),
        state_spec,
        pl.BlockSpec(memory_space=pl.ANY),
    ]
    args = [q_t, k_t, fg_t, w_t, b_t, proj, proj, gain, state, new_state]
    return pl.pallas_call(
        kern,
        grid=(nb, heads),
        in_specs=in_specs,
        out_specs=[pl.BlockSpec((bt, dv), lambda i, h: (i, h)), state_spec],
        out_shape=[jax.ShapeDtypeStruct((bs, vw), F32), jax.ShapeDtypeStruct(state.shape, F32)],
        input_output_aliases={len(args) - 1: 1},
        compiler_params=_params("arbitrary", "arbitrary"),
        name="gla_step",
    )(*args)


def _conv_seq_kernel(cb_ref, cc_ref, ch_ref, w_ref, o_ref, st_ref):
    u = cc_ref[...].astype(F32) * ch_ref[...].astype(F32)
    t = u.shape[0]
    rows = lax.broadcasted_iota(jnp.int32, u.shape, 0)
    u1 = jnp.where(rows >= 1, pltpu.roll(u, 1, 0), 0.0)
    u2 = jnp.where(rows >= 2, pltpu.roll(u, 2, 0), 0.0)
    w = w_ref[...]
    conv = w[0:1] * u2 + w[1:2] * u1 + w[2:3] * u
    o_ref[...] = (cb_ref[...].astype(F32) * conv).astype(o_ref.dtype)
    st_ref[...] = u[t - 2:t]


def _conv_seq(proj, l, conv_w, batch, seq, off, cw):
    assert conv_w.shape[1] == 3 and seq >= 2
    m = proj.shape[0]
    tw = _pick(cw, 512)
    nb = off // tw
    return pl.pallas_call(
        _conv_seq_kernel,
        grid=(batch, cw // tw),
        in_specs=[
            pl.BlockSpec((seq, tw), lambda b, j: (b, nb + j)),
            pl.BlockSpec((seq, tw), lambda b, j: (b, nb + cw // tw + j)),
            pl.BlockSpec((seq, tw), lambda b, j: (b, nb + 2 * (cw // tw) + j)),
            pl.BlockSpec((None, 3, tw), lambda b, j: (l, 0, j)),
        ],
        out_specs=[
            pl.BlockSpec((seq, tw), lambda b, j: (b, j)),
            pl.BlockSpec((None, 2, tw), lambda b, j: (b, 0, j)),
        ],
        out_shape=[jax.ShapeDtypeStruct((m, cw), BF16), jax.ShapeDtypeStruct((batch, 2, cw), F32)],
        compiler_params=_params("arbitrary", "arbitrary"),
        name="conv_seq",
    )(proj, proj, proj, conv_w)


def _conv_step_kernel(cb_ref, cc_ref, ch_ref, s0_ref, s1_ref, w_ref, o_ref, n0_ref, n1_ref):
    u = cc_ref[...] * ch_ref[...]
    w = w_ref[...]
    s1 = s1_ref[...]
    o_ref[...] = cb_ref[...] * (w[0:1] * s0_ref[...] + w[1:2] * s1 + w[2:3] * u)
    n0_ref[...] = s1
    n1_ref[...] = u


def _conv_step(proj, state, l, conv_w, off, cw):
    assert conv_w.shape[1] == 3 and state.shape[2] == 2
    bs = proj.shape[0]
    tw = _pick(cw, 512)
    nb, nw = off // tw, cw // tw
    st = state.reshape(state.shape[0], bs, 2 * cw)
    col = lambda k: pl.BlockSpec((bs, tw), lambda j: (0, k + j))
    stc = lambda k: pl.BlockSpec((None, bs, tw), lambda j: (l, 0, k + j))
    o, n0, n1 = pl.pallas_call(
        _conv_step_kernel,
        grid=(nw,),
        in_specs=[col(nb), col(nb + nw), col(nb + 2 * nw), stc(0), stc(nw),
                  pl.BlockSpec((None, 3, tw), lambda j: (l, 0, j))],
        out_specs=[col(0), col(0), col(0)],
        out_shape=[jax.ShapeDtypeStruct((bs, cw), F32)] * 3,
        compiler_params=_params("arbitrary"),
        name="conv_step",
    )(proj, proj, proj, st, st, conv_w)
    return o, jnp.stack([n0, n1], axis=1)


def _merge_kernel(oa_ref, ob_ref, ga_ref, gb_ref, wa_ref, wb_ref, y_ref):
    ya = _dot(oa_ref[...].astype(BF16), wa_ref[...])
    yb = _dot(ob_ref[...].astype(BF16), wb_ref[...])
    y = jax.nn.sigmoid(ga_ref[...].astype(F32)) * ya + jax.nn.sigmoid(gb_ref[...].astype(F32)) * yb
    y_ref[...] = y.astype(y_ref.dtype)


def _merge(oa, ob, proj, l, w_pa, w_pb, off_ga, tm):
    m, d = oa.shape[0], w_pa.shape[2]
    tn = _pick(d, 1024)
    na, nbk = off_ga // tn, (off_ga + d) // tn
    return pl.pallas_call(
        _merge_kernel,
        grid=(m // tm, d // tn),
        in_specs=[
            pl.BlockSpec((tm, oa.shape[1]), lambda i, j: (i, 0)),
            pl.BlockSpec((tm, ob.shape[1]), lambda i, j: (i, 0)),
            pl.BlockSpec((tm, tn), lambda i, j: (i, na + j)),
            pl.BlockSpec((tm, tn), lambda i, j: (i, nbk + j)),
            pl.BlockSpec((None, w_pa.shape[1], tn), lambda i, j: (l, 0, j)),
            pl.BlockSpec((None, w_pb.shape[1], tn), lambda i, j: (l, 0, j)),
        ],
        out_specs=pl.BlockSpec((tm, tn), lambda i, j: (i, j)),
        out_shape=jax.ShapeDtypeStruct((m, d), BF16),
        compiler_params=_params("arbitrary", "arbitrary"),
        name="merge",
    )(oa, ob, proj, proj, w_pa, w_pb)


def _top2_route(logits, n_experts):
    lane = lax.broadcasted_iota(jnp.int32, logits.shape, 1).astype(F32)
    lg = jnp.where(lane < n_experts, logits, -jnp.inf)
    m1 = jnp.max(lg, axis=1, keepdims=True)
    i1 = jnp.min(jnp.where(lg == m1, lane, float(LANES)), axis=1, keepdims=True)
    lg2 = jnp.where(lane == i1, -jnp.inf, lg)
    m2 = jnp.max(lg2, axis=1, keepdims=True)
    i2 = jnp.min(jnp.where(lg2 == m2, lane, float(LANES)), axis=1, keepdims=True)
    e2 = jnp.exp(m2 - m1)
    den = 1.0 + e2
    return jnp.where(lane == 0.0, i1, jnp.where(lane == 1.0, i2, jnp.where(lane == 2.0, 1.0 / den, e2 / den)))


def _wo_kernel(y_ref, x_ref, g1_ref, sh_ref, sc_ref, gain_ref, w_ref, *rest, n_experts):
    if n_experts:
        router_ref, x1_ref, h_ref, route_ref = rest
    else:
        x1_ref, h_ref = rest
    x1 = x_ref[...] + g1_ref[...] * _dot(y_ref[...], w_ref[...])
    x1_ref[...] = x1
    h = _norm_mod(x1, gain_ref[...], sc_ref[...], sh_ref[...])
    h_ref[...] = h.astype(h_ref.dtype)
    if n_experts:
        route_ref[...] = _top2_route(_dot_f32(h, router_ref[...]), n_experts)


def _wo(y, x, mod, l, gain, w_o, router, lm, n_experts, rows_per_group, tm):
    m, d = x.shape
    grp = lambda i: (i * tm) // rows_per_group
    in_specs = [
        pl.BlockSpec((tm, d), lambda i: (i, 0)),
        pl.BlockSpec((tm, d), lambda i: (i, 0)),
        _mod_spec(mod, l, 2, grp, 1), _mod_spec(mod, l, 3, grp, 1), _mod_spec(mod, l, 4, grp, 1),
        pl.BlockSpec((None, 1, d), lambda i: (l, 0, 0)),
        pl.BlockSpec((None, d, d), lambda i: (l, 0, 0)),
    ]
    args = [y, x, mod, mod, mod, gain, w_o]
    out_specs = [pl.BlockSpec((tm, d), lambda i: (i, 0)), pl.BlockSpec((tm, d), lambda i: (i, 0))]
    out_shape = [jax.ShapeDtypeStruct((m, d), F32), jax.ShapeDtypeStruct((m, d), F32 if n_experts else BF16)]
    if n_experts:
        in_specs.append(pl.BlockSpec((None, d, LANES), lambda i: (lm, 0, 0)))
        args.append(router)
        out_specs.append(pl.BlockSpec((tm, LANES), lambda i: (i, 0)))
        out_shape.append(jax.ShapeDtypeStruct((m, LANES), F32))
    return pl.pallas_call(
        functools.partial(_wo_kernel, n_experts=n_experts),
        grid=(m // tm,),
        in_specs=in_specs,
        out_specs=out_specs,
        out_shape=out_shape,
        compiler_params=_params("arbitrary"),
        name="wo",
    )(*args)


def _ffn_kernel(h_ref, x_ref, g2_ref, w1_ref, w3_ref, w2_ref, o_ref, acc):
    f = pl.program_id(1)

    @pl.when(f == 0)
    def _():
        acc[...] = jnp.zeros_like(acc)

    h = h_ref[...]
    hid = _silu(_dot(h, w1_ref[...])) * _dot(h, w3_ref[...])
    acc[...] += _dot(hid.astype(BF16), w2_ref[...])

    @pl.when(f == pl.num_programs(1) - 1)
    def _():
        o_ref[...] = x_ref[...] + g2_ref[...] * acc[...]


def _ffn(h, x, mod, l, ld, w1, w3, w2, rows_per_group, tm):
    m, d = x.shape
    ff = w1.shape[2]
    tf = _pick(ff, 512)
    grp = lambda i: (i * tm) // rows_per_group
    return pl.pallas_call(
        _ffn_kernel,
        grid=(m // tm, ff // tf),
        in_specs=[
            pl.BlockSpec((tm, d), lambda i, f: (i, 0)),
            pl.BlockSpec((tm, d), lambda i, f: (i, 0)),
            _mod_spec(mod, l, 5, grp, 2),
            pl.BlockSpec((None, d, tf), lambda i, f: (ld, 0, f)),
            pl.BlockSpec((None, d, tf), lambda i, f: (ld, 0, f)),
            pl.BlockSpec((None, tf, d), lambda i, f: (ld, f, 0)),
        ],
        out_specs=pl.BlockSpec((tm, d), lambda i, f: (i, 0)),
        out_shape=jax.ShapeDtypeStruct((m, d), F32),
        scratch_shapes=[pltpu.VMEM((tm, d), F32)],
        compiler_params=_params("arbitrary", "arbitrary"),
        name="ffn",
    )(h, x, mod, w1, w3, w2)


def _route_tables(route, n_experts, tm):
    m = route.shape[0]
    p_rows = -(-(TOP_K * m + n_experts * (tm - 1)) // tm) * tm
    e_flat = jnp.concatenate([route[:, 0], route[:, 1]]).astype(I32)
    onehot = (e_flat[:, None] == jnp.arange(n_experts, dtype=I32)[None, :]).astype(I32)
    rank = jnp.sum((jnp.cumsum(onehot, axis=0) - 1) * onehot, axis=1)
    counts = jnp.sum(onehot, axis=0)
    padded = ((counts + tm - 1) // tm) * tm
    ends = jnp.cumsum(padded)
    dest = (jnp.sum(onehot * (ends - padded)[None, :], axis=1) + rank).astype(I32)
    tile_start = jnp.arange(p_rows // tm, dtype=I32) * tm
    tile_expert = jnp.minimum(jnp.sum((tile_start[:, None] >= ends[None, :]).astype(I32), axis=1), n_experts - 1)
    n_used = (ends[-1] // tm).astype(I32).reshape(1)
    src = jnp.zeros((p_rows,), I32).at[dest].set(jnp.tile(jnp.arange(m, dtype=I32), TOP_K))
    return src, tile_expert.astype(I32), n_used, dest


def _row_copy(table_ref, row, dst_ref, r, sem):
    return pltpu.make_async_copy(table_ref.at[pl.ds(row, 1), :], dst_ref.at[pl.ds(r, 1), :], sem)


def _gather_rows(idx_ref, base, n, table_ref, dst_ref, sem):
    def start(r, carry):
        _row_copy(table_ref, idx_ref[base + r], dst_ref, r, sem).start()
        return carry

    def wait(r, carry):
        _row_copy(table_ref, 0, dst_ref, r, sem).wait()
        return carry

    lax.fori_loop(0, n, start, 0, unroll=8)
    lax.fori_loop(0, n, wait, 0, unroll=8)


def _dispatch_kernel(src_ref, nu_ref, h_hbm, o_ref, sem):
    t = pl.program_id(0)
    tm = o_ref.shape[0]

    @pl.when(t < nu_ref[0])
    def _():
        _gather_rows(src_ref, t * tm, tm, h_hbm, o_ref, sem)

    @pl.when(t >= nu_ref[0])
    def _():
        o_ref[...] = jnp.zeros_like(o_ref)


def _dispatch(h, src, n_used, tm):
    d = h.shape[1]
    p_rows = src.shape[0]
    return pl.pallas_call(
        _dispatch_kernel,
        grid_spec=pltpu.PrefetchScalarGridSpec(
            num_scalar_prefetch=2,
            grid=(p_rows // tm,),
            in_specs=[pl.BlockSpec(memory_space=pl.ANY)],
            out_specs=pl.BlockSpec((tm, d), lambda t, src, nu: (t, 0)),
            scratch_shapes=[pltpu.SemaphoreType.DMA(())],
        ),
        out_shape=jax.ShapeDtypeStruct((p_rows, d), h.dtype),
        compiler_params=_params("arbitrary"),
        name="moe_dispatch",
    )(src, n_used, h)


def _ffn_grouped_kernel(te_ref, nu_ref, x_ref, w1_ref, w3_ref, w2_ref, o_ref, xb, acc):
    t, f = pl.program_id(0), pl.program_id(1)
    used = t < nu_ref[0]
    last = f == pl.num_programs(1) - 1

    @pl.when(used)
    def _():
        @pl.when(f == 0)
        def _():
            xb[...] = x_ref[...].astype(BF16)
            acc[...] = jnp.zeros_like(acc)

        x = xb[...]
        hid = _silu(_dot(x, w1_ref[...])) * _dot(x, w3_ref[...])
        acc[...] += _dot(hid.astype(BF16), w2_ref[...])

    @pl.when(used & last)
    def _():
        o_ref[...] = acc[...]

    @pl.when(jnp.logical_not(used) & last)
    def _():
        o_ref[...] = jnp.zeros_like(o_ref)


def _ffn_grouped(xs, tile_expert, n_used, lm, w1, w3, w2, tm):
    p_rows, d = xs.shape
    ff = w1.shape[3]
    tf = _pick(ff, 512)
    nf = ff // tf
    fidx = lambda t, f, nu: jnp.where(t < nu[0], f, nf - 1)
    return pl.pallas_call(
        _ffn_grouped_kernel,
        grid_spec=pltpu.PrefetchScalarGridSpec(
            num_scalar_prefetch=2,
            grid=(p_rows // tm, nf),
            in_specs=[
                pl.BlockSpec((tm, d), lambda t, f, te, nu: (t, 0)),
                pl.BlockSpec((None, None, d, tf), lambda t, f, te, nu: (lm, te[t], 0, fidx(t, f, nu))),
                pl.BlockSpec((None, None, d, tf), lambda t, f, te, nu: (lm, te[t], 0, fidx(t, f, nu))),
                pl.BlockSpec((None, None, tf, d), lambda t, f, te, nu: (lm, te[t], fidx(t, f, nu), 0)),
            ],
            out_specs=pl.BlockSpec((tm, d), lambda t, f, te, nu: (t, 0)),
            scratch_shapes=[pltpu.VMEM((tm, d), BF16), pltpu.VMEM((tm, d), F32)],
        ),
        out_shape=jax.ShapeDtypeStruct((p_rows, d), F32),
        compiler_params=_params("arbitrary", "arbitrary"),
        name="moe_ffn",
    )(tile_expert, n_used, xs, w1, w3, w2)


def _combine_kernel(dest_ref, x_ref, g2_ref, route_ref, ys_hbm, o_ref, ybuf, sem, *, row0, m_all):
    tm = x_ref.shape[0]
    base = row0 + pl.program_id(0) * tm
    for slot in range(TOP_K):
        _gather_rows(dest_ref, slot * m_all + base, tm, ys_hbm, ybuf.at[slot], sem)
    route = route_ref[...]
    f = route[:, 2:3] * ybuf[0] + route[:, 3:4] * ybuf[1]
    o_ref[...] = x_ref[...] + g2_ref[...] * f


def _combine(x1, mod, l, route, ys, dest, row0, m_all, rows_per_group, tm):
    m, d = x1.shape
    grp = lambda i: (i * tm) // rows_per_group
    r = mod.shape[2]
    return pl.pallas_call(
        functools.partial(_combine_kernel, row0=row0, m_all=m_all),
        grid_spec=pltpu.PrefetchScalarGridSpec(
            num_scalar_prefetch=1,
            grid=(m // tm,),
            in_specs=[
                pl.BlockSpec((tm, d), lambda i, dest: (i, 0)),
                pl.BlockSpec((None, None, r, d), lambda i, dest: (l, grp(i), 0, 5)),
                pl.BlockSpec((tm, LANES), lambda i, dest: (i, 0)),
                pl.BlockSpec(memory_space=pl.ANY),
            ],
            out_specs=pl.BlockSpec((tm, d), lambda i, dest: (i, 0)),
            scratch_shapes=[pltpu.VMEM((TOP_K, tm, d), F32), pltpu.SemaphoreType.DMA(())],
        ),
        out_shape=jax.ShapeDtypeStruct((m, d), F32),
        compiler_params=_params("arbitrary"),
        name="moe_combine",
    )(dest, x1, mod, route, ys)


def _final_norm_kernel(x_ref, gain_ref, o_ref):
    x = x_ref[...]
    o_ref[...] = x * lax.rsqrt(jnp.mean(x * x, axis=-1, keepdims=True) + EPS) * gain_ref[...]


def _final_norm(x, gain, tm):
    m, d = x.shape
    return pl.pallas_call(
        _final_norm_kernel,
        grid=(m // tm,),
        in_specs=[pl.BlockSpec((tm, d), lambda i: (i, 0)), pl.BlockSpec((1, d), lambda i: (0, 0))],
        out_specs=pl.BlockSpec((tm, d), lambda i: (i, 0)),
        out_shape=jax.ShapeDtypeStruct((m, d), F32),
        compiler_params=_params("arbitrary"),
        name="final_norm",
    )(x, gain)


def kernel(x_prompt, x_sample, state_gla, state_conv, c_prompt, c_sample, w_ada, b_ada, norm1, norm2, w_in, w_fg2,
           b_fg2, gla_gain, conv_w, w_pa, w_pb, w_o, dense_w1, dense_w3, dense_w2, router, moe_w1, moe_w3, moe_w2,
           final_norm):
    depth, d = norm1.shape
    nb_p, seq, _ = x_prompt.shape
    nb_s = x_sample.shape[0]
    assert x_sample.shape[1] == 1
    _, _, heads, dk, dv = state_gla.shape
    kw, vw = heads * dk, heads * dv
    rank = w_fg2.shape[1]
    cw = conv_w.shape[-1]
    n_experts = router.shape[-1]
    assert rank <= LANES and n_experts <= LANES and TOP_K == 2
    m_p, m_s = nb_p * seq, nb_s
    off_conv = 2 * kw + 2 * vw
    off_ga = off_conv + 3 * cw

    w_main, w_fg = _stage_w_in(w_in, off_conv, rank)
    w_fg2p = jnp.pad(w_fg2, ((0, 0), (0, LANES - rank), (0, 0)))
    b_fg2r = b_fg2.reshape(depth, 1, kw)
    gain_r = gla_gain.reshape(depth, 1, dv)
    norm1r, norm2r = norm1.reshape(depth, 1, d), norm2.reshape(depth, 1, d)
    w_pab, w_pbb, w_ob = w_pa.astype(BF16), w_pb.astype(BF16), w_o.astype(BF16)
    d_w1, d_w3, d_w2 = dense_w1.astype(BF16), dense_w3.astype(BF16), dense_w2.astype(BF16)
    m_w1, m_w3, m_w2 = moe_w1.astype(BF16), moe_w3.astype(BF16), moe_w2.astype(BF16)
    router_p = jnp.pad(router, ((0, 0), (0, 0), (0, LANES - n_experts)))

    c_all = jnp.concatenate([c_prompt, c_sample], axis=0)
    mod = _adaln(jnp.pad(c_all, ((0, (-c_all.shape[0]) % 16), (0, 0))), w_ada, b_ada)
    mod_p = mod[:, :nb_p].reshape(depth, nb_p, 1, 6 * d)
    mod_s = mod[:, nb_p:nb_p + nb_s].reshape(depth, 1, nb_s, 6 * d)

    tm_p = _pick(seq, 512)
    tm_wo = _pick(seq, 256)
    xp, xs = x_prompt.reshape(m_p, d), x_sample.reshape(m_s, d)
    gla_p, conv_p, conv_s = [], [], []
    gla_s = jnp.zeros(state_gla.shape, F32)
    for l in range(depth):
        proj_p, fg_p = _win(xp, mod_p, l, norm1r, w_main, w_fg, seq, tm_p, BF16)
        proj_s, fg_s = _win(xs, mod_s, l, norm1r, w_main, w_fg, m_s, m_s, F32)
        oa_p, sg = _gla_chunk(proj_p, fg_p, l, w_fg2p, b_fg2r, gain_r, nb_p, seq, heads, dk, dv, _pick(seq, 256))
        gla_p.append(sg)
        oa_s, gla_s = _gla_step(proj_s, fg_s, state_gla, gla_s, l, w_fg2p, b_fg2r, gain_r, heads, dk, dv, rank,
                                  bt=_pick(m_s, 16))
        ob_p, sc = _conv_seq(proj_p, l, conv_w, nb_p, seq, off_conv, cw)
        conv_p.append(sc)
        ob_s, sc = _conv_step(proj_s, state_conv, l, conv_w, off_conv, cw)
        conv_s.append(sc)
        y_p = _merge(oa_p, ob_p, proj_p, l, w_pab, w_pbb, off_ga, tm_p)
        y_s = _merge(oa_s, ob_s, proj_s, l, w_pab, w_pbb, off_ga, m_s)
        if l % 2 == 0:
            x1_p, h_p = _wo(y_p, xp, mod_p, l, norm2r, w_ob, None, 0, 0, seq, tm_wo)
            x1_s, h_s = _wo(y_s, xs, mod_s, l, norm2r, w_ob, None, 0, 0, m_s, m_s)
            xp = _ffn(h_p, x1_p, mod_p, l, l // 2, d_w1, d_w3, d_w2, seq, tm_p)
            xs = _ffn(h_s, x1_s, mod_s, l, l // 2, d_w1, d_w3, d_w2, m_s, m_s)
        else:
            lm = l // 2
            x1_p, h_p, route_p = _wo(y_p, xp, mod_p, l, norm2r, w_ob, router_p, lm, n_experts, seq, tm_wo)
            x1_s, h_s, route_s = _wo(y_s, xs, mod_s, l, norm2r, w_ob, router_p, lm, n_experts, m_s, m_s)
            h_all = jnp.concatenate([h_p, h_s], axis=0)
            src, tile_expert, n_used, dest = _route_tables(jnp.concatenate([route_p, route_s], axis=0),
                                                           n_experts, MOE_TILE)
            ys = _ffn_grouped(_dispatch(h_all, src, n_used, MOE_TILE), tile_expert, n_used, lm, m_w1, m_w3, m_w2,
                              MOE_TILE)
            xp = _combine(x1_p, mod_p, l, route_p, ys, dest, 0, m_p + m_s, seq, tm_wo)
            xs = _combine(x1_s, mod_s, l, route_s, ys, dest, m_p, m_p + m_s, m_s, m_s)
    y_p = _final_norm(xp, final_norm.reshape(1, d), tm_p).reshape(nb_p, seq, d)
    y_s = _final_norm(xs, final_norm.reshape(1, d), m_s).reshape(nb_s, 1, d)
    return (y_p, y_s, jnp.stack(gla_p), jnp.stack(conv_p), gla_s, jnp.stack(conv_s))
```

```python
import functools

import numpy as np
import jax
import jax.numpy as jnp
from jax import lax
from jax.experimental import pallas as pl
from jax.experimental.pallas import tpu as pltpu

F32 = jnp.float32
BF16 = jnp.bfloat16
I32 = jnp.int32
EPS = 1e-6
GATE_TEMP = 16.0
TOP_K = 2
LANES = 128
VMEM_LIMIT_BYTES = 56 * 1024 * 1024
MOE_TILE = 512


def _params(*sem):
    return pltpu.CompilerParams(dimension_semantics=sem, vmem_limit_bytes=VMEM_LIMIT_BYTES)


def _dot(a, b):
    return jnp.dot(a, b, preferred_element_type=F32)


def _dot_nt(a, b):
    return lax.dot_general(a, b, (((1,), (1,)), ((), ())), preferred_element_type=F32)


def _split(x):
    hi = x.astype(BF16)
    lo = (x - hi.astype(F32)).astype(BF16)
    return hi, lo


def _dot_f32(a, b):
    ah, al = _split(a)
    bh, bl = _split(b)
    return _dot(ah, bh) + (_dot(ah, bl) + _dot(al, bh))


def _silu(x):
    return x * jax.nn.sigmoid(x)


def _log_sigmoid(x):
    return jnp.minimum(x, 0.0) - jnp.log1p(jnp.exp(-jnp.abs(x)))


def _norm_mod(x, gain, scale, shift):
    y = x * lax.rsqrt(jnp.mean(x * x, axis=-1, keepdims=True) + EPS) * gain
    return y * (1.0 + scale) + shift


def _pick(n, pref):
    t = min(n, pref)
    while n % t:
        t //= 2
    return t


def _mod_spec(mod, l, k, grp, nargs):
    r, d = mod.shape[2], mod.shape[3] // 6
    if nargs == 1:
        return pl.BlockSpec((None, None, r, d), lambda i: (l, grp(i), 0, k))
    return pl.BlockSpec((None, None, r, d), lambda i, j: (l, grp(i), 0, k))


def _adaln_kernel(c_ref, w_ref, b_ref, o_ref):
    a = _silu(c_ref[...]).astype(BF16)
    o_ref[...] = _dot(a, w_ref[...].astype(BF16)) + b_ref[...]


def _adaln(c, w_ada, b_ada):
    depth, d, n = w_ada.shape
    r = c.shape[0]
    tn = _pick(n, 1024)
    return pl.pallas_call(
        _adaln_kernel,
        grid=(depth, n // tn),
        in_specs=[
            pl.BlockSpec((r, d), lambda l, j: (0, 0)),
            pl.BlockSpec((None, d, tn), lambda l, j: (l, 0, j)),
            pl.BlockSpec((None, 1, tn), lambda l, j: (l, 0, j)),
        ],
        out_specs=pl.BlockSpec((None, r, tn), lambda l, j: (l, 0, j)),
        out_shape=jax.ShapeDtypeStruct((depth, r, n), F32),
        compiler_params=_params("arbitrary", "arbitrary"),
        name="adaln",
    )(c, w_ada, b_ada.reshape(depth, 1, n))


def _stage_w_in_kernel(wt_ref, fgt_ref, main_ref, fg_ref, *, rank):
    main_ref[...] = jnp.transpose(wt_ref[0]).astype(BF16)

    @pl.when(pl.program_id(1) == 0)
    def _():
        fg = jnp.transpose(fgt_ref[...])
        lane = lax.broadcasted_iota(jnp.int32, fg.shape, 1)
        fg_ref[...] = jnp.where(lane < rank, fg, 0.0).astype(BF16)


def _stage_w_in(w_in, off, rank):
    depth, d, n = w_in.shape
    w_t = jnp.swapaxes(w_in, 1, 2)
    tc = _pick(int(np.gcd(off, n - rank - off)), 512)
    assert off % LANES == 0 and rank % 8 == 0
    src_row = lambda j: pl.multiple_of(j * tc + jnp.where(j * tc >= off, rank, 0), 8)
    return pl.pallas_call(
        functools.partial(_stage_w_in_kernel, rank=rank),
        grid=(depth, (n - rank) // tc),
        in_specs=[
            pl.BlockSpec((pl.Element(1), pl.Element(tc), pl.Element(d)), lambda l, j: (l, src_row(j), 0)),
            pl.BlockSpec((None, LANES, d), lambda l, j: (l, off // LANES, 0)),
        ],
        out_specs=[
            pl.BlockSpec((None, d, tc), lambda l, j: (l, 0, j)),
            pl.BlockSpec((None, d, LANES), lambda l, j: (l, 0, 0)),
        ],
        out_shape=[jax.ShapeDtypeStruct((depth, d, n - rank), BF16), jax.ShapeDtypeStruct((depth, d, LANES), BF16)],
        compiler_params=_params("arbitrary", "arbitrary"),
        name="stage_w_in",
    )(w_t, w_t)


def _win_kernel(x_ref, sh_ref, sc_ref, gain_ref, w_ref, wfg_ref, proj_ref, fg_ref, h_scr):
    @pl.when(pl.program_id(1) == 0)
    def _():
        h = _norm_mod(x_ref[...], gain_ref[...], sc_ref[...], sh_ref[...]).astype(BF16)
        h_scr[...] = h
        fg_ref[...] = _dot(h, wfg_ref[...])

    proj_ref[...] = _dot(h_scr[...], w_ref[...]).astype(proj_ref.dtype)


def _win(x, mod, l, gain, w_main, w_fg, rows_per_group, tm, out_dtype):
    m, d = x.shape
    n = w_main.shape[2]
    tn = _pick(n, 2048)
    grp = lambda i: (i * tm) // rows_per_group
    return pl.pallas_call(
        _win_kernel,
        grid=(m // tm, n // tn),
        in_specs=[
            pl.BlockSpec((tm, d), lambda i, j: (i, 0)),
            _mod_spec(mod, l, 0, grp, 2),
            _mod_spec(mod, l, 1, grp, 2),
            pl.BlockSpec((None, 1, d), lambda i, j: (l, 0, 0)),
            pl.BlockSpec((None, d, tn), lambda i, j: (l, 0, j)),
            pl.BlockSpec((None, d, LANES), lambda i, j: (l, 0, 0)),
        ],
        out_specs=[
            pl.BlockSpec((tm, tn), lambda i, j: (i, j)),
            pl.BlockSpec((tm, LANES), lambda i, j: (i, 0)),
        ],
        out_shape=[jax.ShapeDtypeStruct((m, n), out_dtype), jax.ShapeDtypeStruct((m, LANES), F32)],
        scratch_shapes=[pltpu.VMEM((tm, d), BF16)],
        compiler_params=_params("arbitrary", "arbitrary"),
        name="win",
    )(x, mod, mod, gain, w_main, w_fg)


GLA_VPU_LEVEL_MIN = 16


def _gla_tables(c):
    i = np.arange(c)[:, None]
    j = np.arange(c)[None, :]
    coefs = [(j <= i)]
    masks = [(i == j)]
    sizes = []
    b = c
    while b >= 2:
        mid = (i // b) * b + b // 2 - 1
        if b < GLA_VPU_LEVEL_MIN:
            coefs.append(((j > mid) & (j <= i)) | ((j > i) & (j <= mid)))
        masks.append((i // b == j // b) & (i % b >= b // 2) & (j % b < b // 2))
        sizes.append(b)
        b //= 2
    return np.concatenate(coefs, 0).astype(np.float32), np.stack(masks).astype(np.float32), tuple(sizes)


def _gla_chunk_kernel(q_ref, k_ref, v_ref, g_ref, fg_ref, wfg2_ref, bfg_ref, gain_ref, coef_ref, mask_ref,
                      o_ref, s_out_ref, s_scr, *, scale, chunk, sizes, heads):
    ci = pl.program_id(1)

    @pl.when(ci == 0)
    def _():
        s_scr[...] = jnp.zeros_like(s_scr)

    c = chunk
    dk, dv = s_scr.shape[1], s_scr.shape[2]
    log_a_all = _log_sigmoid(_dot_f32(fg_ref[...], wfg2_ref[...]) + bfg_ref[...]) * (1.0 / GATE_TEMP)
    coef = coef_ref[...]
    nt = min(c, LANES)
    for h in range(heads):
        log_a = log_a_all[:, h * dk:(h + 1) * dk]
        la_hi, la_mid = _split(log_a)
        la_lo = (log_a - la_hi.astype(F32) - la_mid.astype(F32)).astype(BF16)
        expo = _dot(coef, la_hi) + (_dot(coef, la_mid) + _dot(coef, la_lo))
        cum = expo[0:c]
        d_cum = jnp.exp(cum)
        d_tail = jnp.exp(cum[c - 1:c] - cum)

        q = q_ref[:, h * dk:(h + 1) * dk].astype(F32) * scale
        k = k_ref[:, h * dk:(h + 1) * dk].astype(F32)
        v = v_ref[:, h * dv:(h + 1) * dv].astype(BF16)

        scores = mask_ref[0] * _dot_nt(q.astype(BF16), k.astype(BF16))
        n_small = 0
        for lv, b in enumerate(sizes):
            if b >= GLA_VPU_LEVEL_MIN:
                blocks = cum.reshape(c // b, b, dk)
                d_lv = jnp.exp(-jnp.abs(blocks - blocks[:, b // 2 - 1:b // 2, :])).reshape(c, dk)
            else:
                n_small += 1
                d_lv = jnp.exp(expo[n_small * c:(n_small + 1) * c])
            scores = scores + mask_ref[1 + lv] * _dot_nt((q * d_lv).astype(BF16), (k * d_lv).astype(BF16))

        s = s_scr[h]
        o = _dot(scores.astype(BF16), v) + _dot((q * d_cum).astype(BF16), s.astype(BF16))

        a_col = jnp.transpose(d_cum[c - nt:c])[:, nt - 1:nt]
        k_tail_t = jnp.transpose(k * d_tail).astype(BF16)
        s_scr[h] = a_col * s + _dot(k_tail_t, v)

        o = o * lax.rsqrt(jnp.mean(o * o, axis=-1, keepdims=True) + EPS) * gain_ref[...]
        o_ref[:, h * dv:(h + 1) * dv] = (o * _silu(g_ref[:, h * dv:(h + 1) * dv].astype(F32))).astype(o_ref.dtype)

    @pl.when(ci == pl.num_programs(1) - 1)
    def _():
        s_out_ref[...] = s_scr[...]


def _gla_chunk(proj, fg, l, w_fg2, b_fg2, gain, batch, seq, heads, dk, dv, chunk):
    m = proj.shape[0]
    kw, vw = heads * dk, heads * dv
    nc = seq // chunk
    coef, mask, sizes = _gla_tables(chunk)
    rank_pad = w_fg2.shape[1]
    row = lambda b, c: b * nc + c
    kern = functools.partial(_gla_chunk_kernel, scale=float(dk) ** -0.5, chunk=chunk, sizes=sizes, heads=heads)
    return pl.pallas_call(
        kern,
        grid=(batch, nc),
        in_specs=[
            pl.BlockSpec((chunk, kw), lambda b, c: (row(b, c), 0)),
            pl.BlockSpec((chunk, kw), lambda b, c: (row(b, c), 1)),
            pl.BlockSpec((chunk, vw), lambda b, c: (row(b, c), (2 * kw) // vw)),
            pl.BlockSpec((chunk, vw), lambda b, c: (row(b, c), (2 * kw) // vw + 1)),
            pl.BlockSpec((chunk, LANES), lambda b, c: (row(b, c), 0)),
            pl.BlockSpec((None, rank_pad, kw), lambda b, c: (l, 0, 0)),
            pl.BlockSpec((None, 1, kw), lambda b, c: (l, 0, 0)),
            pl.BlockSpec((None, 1, dv), lambda b, c: (l, 0, 0)),
            pl.BlockSpec(coef.shape, lambda b, c: (0, 0)),
            pl.BlockSpec(mask.shape, lambda b, c: (0, 0, 0)),
        ],
        out_specs=[
            pl.BlockSpec((chunk, vw), lambda b, c: (row(b, c), 0)),
            pl.BlockSpec((None, heads, dk, dv), lambda b, c: (b, 0, 0, 0)),
        ],
        out_shape=[jax.ShapeDtypeStruct((m, vw), BF16), jax.ShapeDtypeStruct((batch, heads, dk, dv), F32)],
        scratch_shapes=[pltpu.VMEM((heads, dk, dv), F32)],
        compiler_params=_params("arbitrary", "arbitrary"),
        name="gla_chunk",
    )(proj, proj, proj, proj, fg, w_fg2, b_fg2, gain, jnp.asarray(coef, BF16), jnp.asarray(mask, F32))


def _gla_step_kernel(qt_ref, kt_ref, fgt_ref, wfg2t_ref, bfgt_ref, v_ref, g_ref, gain_ref, s_ref, carry_ref,
                     o_ref, s_out_ref, *, scale, bt, rank):
    del carry_ref
    w_t = wfg2t_ref[...]
    fg_t = fgt_ref[...]
    xg = bfgt_ref[...]
    for r in range(rank):
        xg = xg + w_t[:, r:r + 1] * fg_t[r:r + 1, :]
    a_t = jnp.exp(_log_sigmoid(xg) * (1.0 / GATE_TEMP))
    q_t = qt_ref[...] * scale
    k_t = kt_ref[...]
    for j in range(bt):
        s_new = a_t[:, j:j + 1] * s_ref[j] + k_t[:, j:j + 1] * v_ref[j:j + 1, :]
        s_out_ref[j] = s_new
        o = jnp.sum(q_t[:, j:j + 1] * s_new, axis=0, keepdims=True)
        o = o * lax.rsqrt(jnp.mean(o * o, axis=-1, keepdims=True) + EPS) * gain_ref[...]
        o_ref[j:j + 1, :] = o * _silu(g_ref[j:j + 1, :])


def _gla_step(proj, fg, state, new_state, l, w_fg2, b_fg2, gain, heads, dk, dv, rank, bt):
    bs = proj.shape[0]
    kw, vw = heads * dk, heads * dv
    nb = bs // bt
    rank_pad = w_fg2.shape[1]
    to_cols = lambda a: a.reshape(nb, bt, heads, dk).transpose(2, 0, 3, 1)
    q_t = to_cols(proj[:, :kw])
    k_t = to_cols(proj[:, kw:2 * kw])
    fg_t = fg.reshape(nb, bt, rank_pad).transpose(0, 2, 1)
    w_t = w_fg2[l].reshape(rank_pad, heads, dk).transpose(1, 2, 0)
    b_t = b_fg2[l].reshape(heads, dk, 1)
    kern = functools.partial(_gla_step_kernel, scale=float(dk) ** -0.5, bt=bt, rank=rank)
    state_spec = pl.BlockSpec((None, bt, None, dk, dv), lambda i, h: (l, i, h, 0, 0))
    in_specs = [
        pl.BlockSpec((None, None, dk, bt), lambda i, h: (h, i, 0, 0)),
        pl.BlockSpec((None, None, dk, bt), lambda i, h: (h, i, 0, 0)),
        pl.BlockSpec((None, rank_pad, bt), lambda i, h: (i, 0, 0)),
        pl.BlockSpec((None, dk, rank_pad), lambda i, h: (h, 0, 0)),
        pl.BlockSpec((None, dk, 1), lambda i, h: (h, 0, 0)),
        pl.BlockSpec((bt, dv), lambda i, h: (i, (2 * kw) // dv + h)),
        pl.BlockSpec((bt, dv), lambda i, h: (i, (2 * kw + vw) // dv + h)),
        pl.BlockSpec((None, 1, dv), lambda i, h: (l, 0, 0)),
        state_spec,
        pl.BlockSpec(memory_space=pl.ANY),
    ]
    args = [q_t, k_t, fg_t, w_t, b_t, proj, proj, gain, state, new_state]
    return pl.pallas_call(
        kern,
        grid=(nb, heads),
        in_specs=in_specs,
        out_specs=[pl.BlockSpec((bt, dv), lambda i, h: (i, h)), state_spec],
        out_shape=[jax.ShapeDtypeStruct((bs, vw), F32), jax.ShapeDtypeStruct(state.shape, F32)],
        input_output_aliases={len(args) - 1: 1},
        compiler_params=_params("arbitrary", "arbitrary"),
        name="gla_step",
    )(*args)


def _conv_seq_kernel(cb_ref, cc_ref, ch_ref, w_ref, o_ref, st_ref):
    u = cc_ref[...].astype(F32) * ch_ref[...].astype(F32)
    t = u.shape[0]
    rows = lax.broadcasted_iota(jnp.int32, u.shape, 0)
    u1 = jnp.where(rows >= 1, pltpu.roll(u, 1, 0), 0.0)
    u2 = jnp.where(rows >= 2, pltpu.roll(u, 2, 0), 0.0)
    w = w_ref[...]
    conv = w[0:1] * u2 + w[1:2] * u1 + w[2:3] * u
    o_ref[...] = (cb_ref[...].astype(F32) * conv).astype(o_ref.dtype)
    st_ref[...] = u[t - 2:t]


def _conv_seq(proj, l, conv_w, batch, seq, off, cw):
    assert conv_w.shape[1] == 3 and seq >= 2
    m = proj.shape[0]
    tw = _pick(cw, 512)
    nb = off // tw
    return pl.pallas_call(
        _conv_seq_kernel,
        grid=(batch, cw // tw),
        in_specs=[
            pl.BlockSpec((seq, tw), lambda b, j: (b, nb + j)),
            pl.BlockSpec((seq, tw), lambda b, j: (b, nb + cw // tw + j)),
            pl.BlockSpec((seq, tw), lambda b, j: (b, nb + 2 * (cw // tw) + j)),
            pl.BlockSpec((None, 3, tw), lambda b, j: (l, 0, j)),
        ],
        out_specs=[
            pl.BlockSpec((seq, tw), lambda b, j: (b, j)),
            pl.BlockSpec((None, 2, tw), lambda b, j: (b, 0, j)),
        ],
        out_shape=[jax.ShapeDtypeStruct((m, cw), BF16), jax.ShapeDtypeStruct((batch, 2, cw), F32)],
        compiler_params=_params("arbitrary", "arbitrary"),
        name="conv_seq",
    )(proj, proj, proj, conv_w)


def _conv_step_kernel(cb_ref, cc_ref, ch_ref, s0_ref, s1_ref, w_ref, o_ref, n0_ref, n1_ref):
    u = cc_ref[...] * ch_ref[...]
    w = w_ref[...]
    s1 = s1_ref[...]
    o_ref[...] = cb_ref[...] * (w[0:1] * s0_ref[...] + w[1:2] * s1 + w[2:3] * u)
    n0_ref[...] = s1
    n1_ref[...] = u


def _conv_step(proj, state, l, conv_w, off, cw):
    assert conv_w.shape[1] == 3 and state.shape[2] == 2
    bs = proj.shape[0]
    tw = _pick(cw, 512)
    nb, nw = off // tw, cw // tw
    st = state.reshape(state.shape[0], bs, 2 * cw)
    col = lambda k: pl.BlockSpec((bs, tw), lambda j: (0, k + j))
    stc = lambda k: pl.BlockSpec((None, bs, tw), lambda j: (l, 0, k + j))
    o, n0, n1 = pl.pallas_call(
        _conv_step_kernel,
        grid=(nw,),
        in_specs=[col(nb), col(nb + nw), col(nb + 2 * nw), stc(0), stc(nw),
                  pl.BlockSpec((None, 3, tw), lambda j: (l, 0, j))],
        out_specs=[col(0), col(0), col(0)],
        out_shape=[jax.ShapeDtypeStruct((bs, cw), F32)] * 3,
        compiler_params=_params("arbitrary"),
        name="conv_step",
    )(proj, proj, proj, st, st, conv_w)
    return o, jnp.stack([n0, n1], axis=1)


def _merge_kernel(oa_ref, ob_ref, ga_ref, gb_ref, wa_ref, wb_ref, y_ref):
    ya = _dot(oa_ref[...].astype(BF16), wa_ref[...])
    yb = _dot(ob_ref[...].astype(BF16), wb_ref[...])
    y = jax.nn.sigmoid(ga_ref[...].astype(F32)) * ya + jax.nn.sigmoid(gb_ref[...].astype(F32)) * yb
    y_ref[...] = y.astype(y_ref.dtype)


def _merge(oa, ob, proj, l, w_pa, w_pb, off_ga, tm):
    m, d = oa.shape[0], w_pa.shape[2]
    tn = _pick(d, 1024)
    na, nbk = off_ga // tn, (off_ga + d) // tn
    return pl.pallas_call(
        _merge_kernel,
        grid=(m // tm, d // tn),
        in_specs=[
            pl.BlockSpec((tm, oa.shape[1]), lambda i, j: (i, 0)),
            pl.BlockSpec((tm, ob.shape[1]), lambda i, j: (i, 0)),
            pl.BlockSpec((tm, tn), lambda i, j: (i, na + j)),
            pl.BlockSpec((tm, tn), lambda i, j: (i, nbk + j)),
            pl.BlockSpec((None, w_pa.shape[1], tn), lambda i, j: (l, 0, j)),
            pl.BlockSpec((None, w_pb.shape[1], tn), lambda i, j: (l, 0, j)),
        ],
        out_specs=pl.BlockSpec((tm, tn), lambda i, j: (i, j)),
        out_shape=jax.ShapeDtypeStruct((m, d), BF16),
        compiler_params=_params("arbitrary", "arbitrary"),
        name="merge",
    )(oa, ob, proj, proj, w_pa, w_pb)


def _top2_route(logits, n_experts):
    lane = lax.broadcasted_iota(jnp.int32, logits.shape, 1).astype(F32)
    lg = jnp.where(lane < n_experts, logits, -jnp.inf)
    m1 = jnp.max(lg, axis=1, keepdims=True)
    i1 = jnp.min(jnp.where(lg == m1, lane, float(LANES)), axis=1, keepdims=True)
    lg2 = jnp.where(lane == i1, -jnp.inf, lg)
    m2 = jnp.max(lg2, axis=1, keepdims=True)
    i2 = jnp.min(jnp.where(lg2 == m2, lane, float(LANES)), axis=1, keepdims=True)
    e2 = jnp.exp(m2 - m1)
    den = 1.0 + e2
    return jnp.where(lane == 0.0, i1, jnp.where(lane == 1.0, i2, jnp.where(lane == 2.0, 1.0 / den, e2 / den)))


def _wo_kernel(y_ref, x_ref, g1_ref, sh_ref, sc_ref, gain_ref, w_ref, *rest, n_experts):
    if n_experts:
        router_ref, x1_ref, h_ref, route_ref = rest
    else:
        x1_ref, h_ref = rest
    x1 = x_ref[...] + g1_ref[...] * _dot(y_ref[...], w_ref[...])
    x1_ref[...] = x1
    h = _norm_mod(x1, gain_ref[...], sc_ref[...], sh_ref[...])
    h_ref[...] = h.astype(h_ref.dtype)
    if n_experts:
        route_ref[...] = _top2_route(_dot_f32(h, router_ref[...]), n_experts)


def _wo(y, x, mod, l, gain, w_o, router, lm, n_experts, rows_per_group, tm):
    m, d = x.shape
    grp = lambda i: (i * tm) // rows_per_group
    in_specs = [
        pl.BlockSpec((tm, d), lambda i: (i, 0)),
        pl.BlockSpec((tm, d), lambda i: (i, 0)),
        _mod_spec(mod, l, 2, grp, 1), _mod_spec(mod, l, 3, grp, 1), _mod_spec(mod, l, 4, grp, 1),
        pl.BlockSpec((None, 1, d), lambda i: (l, 0, 0)),
        pl.BlockSpec((None, d, d), lambda i: (l, 0, 0)),
    ]
    args = [y, x, mod, mod, mod, gain, w_o]
    out_specs = [pl.BlockSpec((tm, d), lambda i: (i, 0)), pl.BlockSpec((tm, d), lambda i: (i, 0))]
    out_shape = [jax.ShapeDtypeStruct((m, d), F32), jax.ShapeDtypeStruct((m, d), F32 if n_experts else BF16)]
    if n_experts:
        in_specs.append(pl.BlockSpec((None, d, LANES), lambda i: (lm, 0, 0)))
        args.append(router)
        out_specs.append(pl.BlockSpec((tm, LANES), lambda i: (i, 0)))
        out_shape.append(jax.ShapeDtypeStruct((m, LANES), F32))
    return pl.pallas_call(
        functools.partial(_wo_kernel, n_experts=n_experts),
        grid=(m // tm,),
        in_specs=in_specs,
        out_specs=out_specs,
        out_shape=out_shape,
        compiler_params=_params("arbitrary"),
        name="wo",
    )(*args)


def _ffn_kernel(h_ref, x_ref, g2_ref, w1_ref, w3_ref, w2_ref, o_ref, acc):
    f = pl.program_id(1)

    @pl.when(f == 0)
    def _():
        acc[...] = jnp.zeros_like(acc)

    h = h_ref[...]
    hid = _silu(_dot(h, w1_ref[...])) * _dot(h, w3_ref[...])
    acc[...] += _dot(hid.astype(BF16), w2_ref[...])

    @pl.when(f == pl.num_programs(1) - 1)
    def _():
        o_ref[...] = x_ref[...] + g2_ref[...] * acc[...]


def _ffn(h, x, mod, l, ld, w1, w3, w2, rows_per_group, tm):
    m, d = x.shape
    ff = w1.shape[2]
    tf = _pick(ff, 512)
    grp = lambda i: (i * tm) // rows_per_group
    return pl.pallas_call(
        _ffn_kernel,
        grid=(m // tm, ff // tf),
        in_specs=[
            pl.BlockSpec((tm, d), lambda i, f: (i, 0)),
            pl.BlockSpec((tm, d), lambda i, f: (i, 0)),
            _mod_spec(mod, l, 5, grp, 2),
            pl.BlockSpec((None, d, tf), lambda i, f: (ld, 0, f)),
            pl.BlockSpec((None, d, tf), lambda i, f: (ld, 0, f)),
            pl.BlockSpec((None, tf, d), lambda i, f: (ld, f, 0)),
        ],
        out_specs=pl.BlockSpec((tm, d), lambda i, f: (i, 0)),
        out_shape=jax.ShapeDtypeStruct((m, d), F32),
        scratch_shapes=[pltpu.VMEM((tm, d), F32)],
        compiler_params=_params("arbitrary", "arbitrary"),
        name="ffn",
    )(h, x, mod, w1, w3, w2)


def _route_tables(route, n_experts, tm):
    m = route.shape[0]
    p_rows = -(-(TOP_K * m + n_experts * (tm - 1)) // tm) * tm
    e_flat = jnp.concatenate([route[:, 0], route[:, 1]]).astype(I32)
    onehot = (e_flat[:, None] == jnp.arange(n_experts, dtype=I32)[None, :]).astype(I32)
    rank = jnp.sum((jnp.cumsum(onehot, axis=0) - 1) * onehot, axis=1)
    counts = jnp.sum(onehot, axis=0)
    padded = ((counts + tm - 1) // tm) * tm
    ends = jnp.cumsum(padded)
    dest = (jnp.sum(onehot * (ends - padded)[None, :], axis=1) + rank).astype(I32)
    tile_start = jnp.arange(p_rows // tm, dtype=I32) * tm
    tile_expert = jnp.minimum(jnp.sum((tile_start[:, None] >= ends[None, :]).astype(I32), axis=1), n_experts - 1)
    n_used = (ends[-1] // tm).astype(I32).reshape(1)
    src = jnp.zeros((p_rows,), I32).at[dest].set(jnp.tile(jnp.arange(m, dtype=I32), TOP_K))
    return src, tile_expert.astype(I32), n_used, dest


def _row_copy(table_ref, row, dst_ref, r, sem):
    return pltpu.make_async_copy(table_ref.at[pl.ds(row, 1), :], dst_ref.at[pl.ds(r, 1), :], sem)


def _start_rows(idx_ref, base, n, table_ref, dst_ref, sem):
    def start(r, carry):
        _row_copy(table_ref, idx_ref[base + r], dst_ref, r, sem).start()
        return carry

    lax.fori_loop(0, n, start, 0, unroll=8)


def _wait_rows(n, table_ref, dst_ref, sem):
    def wait(r, carry):
        _row_copy(table_ref, 0, dst_ref, r, sem).wait()
        return carry

    lax.fori_loop(0, n, wait, 0, unroll=8)


def _gather_rows(idx_ref, base, n, table_ref, dst_ref, sem):
    _start_rows(idx_ref, base, n, table_ref, dst_ref, sem)
    _wait_rows(n, table_ref, dst_ref, sem)


def _ffn_grouped_kernel(src_ref, te_ref, nu_ref, h_hbm, w1_ref, w3_ref, w2_ref, o_ref, xg, xb, acc, sems):
    t, f = pl.program_id(0), pl.program_id(1)
    tm = o_ref.shape[0]
    n_used = nu_ref[0]
    used = t < n_used
    last = f == pl.num_programs(1) - 1

    @pl.when(used & (f == 0))
    def _():
        slot = t % 2

        @pl.when(t == 0)
        def _():
            _start_rows(src_ref, 0, tm, h_hbm, xg.at[0], sems.at[0])

        _wait_rows(tm, h_hbm, xg.at[slot], sems.at[slot])

        @pl.when(t + 1 < n_used)
        def _():
            _start_rows(src_ref, (t + 1) * tm, tm, h_hbm, xg.at[1 - slot], sems.at[1 - slot])

        xb[...] = xg[slot].astype(BF16)
        acc[...] = jnp.zeros_like(acc)

    @pl.when(used)
    def _():
        x = xb[...]
        hid = _silu(_dot(x, w1_ref[...])) * _dot(x, w3_ref[...])
        acc[...] += _dot(hid.astype(BF16), w2_ref[...])

    @pl.when(used & last)
    def _():
        o_ref[...] = acc[...]

    @pl.when(jnp.logical_not(used) & last)
    def _():
        o_ref[...] = jnp.zeros_like(o_ref)


def _ffn_grouped(h, src, tile_expert, n_used, lm, w1, w3, w2, tm):
    d = h.shape[1]
    p_rows = src.shape[0]
    ff = w1.shape[3]
    tf = _pick(ff, 512)
    nf = ff // tf
    fidx = lambda t, f, nu: jnp.where(t < nu[0], f, nf - 1)
    return pl.pallas_call(
        _ffn_grouped_kernel,
        grid_spec=pltpu.PrefetchScalarGridSpec(
            num_scalar_prefetch=3,
            grid=(p_rows // tm, nf),
            in_specs=[
                pl.BlockSpec(memory_space=pl.ANY),
                pl.BlockSpec((None, None, d, tf), lambda t, f, src, te, nu: (lm, te[t], 0, fidx(t, f, nu))),
                pl.BlockSpec((None, None, d, tf), lambda t, f, src, te, nu: (lm, te[t], 0, fidx(t, f, nu))),
                pl.BlockSpec((None, None, tf, d), lambda t, f, src, te, nu: (lm, te[t], fidx(t, f, nu), 0)),
            ],
            out_specs=pl.BlockSpec((tm, d), lambda t, f, src, te, nu: (t, 0)),
            scratch_shapes=[pltpu.VMEM((2, tm, d), h.dtype), pltpu.VMEM((tm, d), BF16), pltpu.VMEM((tm, d), F32),
                            pltpu.SemaphoreType.DMA((2,))],
        ),
        out_shape=jax.ShapeDtypeStruct((p_rows, d), F32),
        compiler_params=_params("arbitrary", "arbitrary"),
        name="moe_ffn",
    )(src, tile_expert, n_used, h, w1, w3, w2)


def _combine_kernel(dest_ref, x_ref, g2_ref, route_ref, ys_hbm, o_ref, ybuf, sem, *, row0, m_all):
    tm = x_ref.shape[0]
    base = row0 + pl.program_id(0) * tm
    for slot in range(TOP_K):
        _gather_rows(dest_ref, slot * m_all + base, tm, ys_hbm, ybuf.at[slot], sem)
    route = route_ref[...]
    f = route[:, 2:3] * ybuf[0] + route[:, 3:4] * ybuf[1]
    o_ref[...] = x_ref[...] + g2_ref[...] * f


def _combine(x1, mod, l, route, ys, dest, row0, m_all, rows_per_group, tm):
    m, d = x1.shape
    grp = lambda i: (i * tm) // rows_per_group
    r = mod.shape[2]
    return pl.pallas_call(
        functools.partial(_combine_kernel, row0=row0, m_all=m_all),
        grid_spec=pltpu.PrefetchScalarGridSpec(
            num_scalar_prefetch=1,
            grid=(m // tm,),
            in_specs=[
                pl.BlockSpec((tm, d), lambda i, dest: (i, 0)),
                pl.BlockSpec((None, None, r, d), lambda i, dest: (l, grp(i), 0, 5)),
                pl.BlockSpec((tm, LANES), lambda i, dest: (i, 0)),
                pl.BlockSpec(memory_space=pl.ANY),
            ],
            out_specs=pl.BlockSpec((tm, d), lambda i, dest: (i, 0)),
            scratch_shapes=[pltpu.VMEM((TOP_K, tm, d), F32), pltpu.SemaphoreType.DMA(())],
        ),
        out_shape=jax.ShapeDtypeStruct((m, d), F32),
        compiler_params=_params("arbitrary"),
        name="moe_combine",
    )(dest, x1, mod, route, ys)


def _final_norm_kernel(x_ref, gain_ref, o_ref):
    x = x_ref[...]
    o_ref[...] = x * lax.rsqrt(jnp.mean(x * x, axis=-1, keepdims=True) + EPS) * gain_ref[...]


def _final_norm(x, gain, tm):
    m, d = x.shape
    return pl.pallas_call(
        _final_norm_kernel,
        grid=(m // tm,),
        in_specs=[pl.BlockSpec((tm, d), lambda i: (i, 0)), pl.BlockSpec((1, d), lambda i: (0, 0))],
        out_specs=pl.BlockSpec((tm, d), lambda i: (i, 0)),
        out_shape=jax.ShapeDtypeStruct((m, d), F32),
        compiler_params=_params("arbitrary"),
        name="final_norm",
    )(x, gain)


def kernel(x_prompt, x_sample, state_gla, state_conv, c_prompt, c_sample, w_ada, b_ada, norm1, norm2, w_in, w_fg2,
           b_fg2, gla_gain, conv_w, w_pa, w_pb, w_o, dense_w1, dense_w3, dense_w2, router, moe_w1, moe_w3, moe_w2,
           final_norm):
    depth, d = norm1.shape
    nb_p, seq, _ = x_prompt.shape
    nb_s = x_sample.shape[0]
    assert x_sample.shape[1] == 1
    _, _, heads, dk, dv = state_gla.shape
    kw, vw = heads * dk, heads * dv
    rank = w_fg2.shape[1]
    cw = conv_w.shape[-1]
    n_experts = router.shape[-1]
    assert rank <= LANES and n_experts <= LANES and TOP_K == 2
    m_p, m_s = nb_p * seq, nb_s
    off_conv = 2 * kw + 2 * vw
    off_ga = off_conv + 3 * cw

    w_main, w_fg = _stage_w_in(w_in, off_conv, rank)
    w_fg2p = jnp.pad(w_fg2, ((0, 0), (0, LANES - rank), (0, 0)))
    b_fg2r = b_fg2.reshape(depth, 1, kw)
    gain_r = gla_gain.reshape(depth, 1, dv)
    norm1r, norm2r = norm1.reshape(depth, 1, d), norm2.reshape(depth, 1, d)
    w_pab, w_pbb, w_ob = w_pa.astype(BF16), w_pb.astype(BF16), w_o.astype(BF16)
    d_w1, d_w3, d_w2 = dense_w1.astype(BF16), dense_w3.astype(BF16), dense_w2.astype(BF16)
    m_w1, m_w3, m_w2 = moe_w1.astype(BF16), moe_w3.astype(BF16), moe_w2.astype(BF16)
    router_p = jnp.pad(router, ((0, 0), (0, 0), (0, LANES - n_experts)))

    c_all = jnp.concatenate([c_prompt, c_sample], axis=0)
    mod = _adaln(jnp.pad(c_all, ((0, (-c_all.shape[0]) % 16), (0, 0))), w_ada, b_ada)
    mod_p = mod[:, :nb_p].reshape(depth, nb_p, 1, 6 * d)
    mod_s = mod[:, nb_p:nb_p + nb_s].reshape(depth, 1, nb_s, 6 * d)

    tm_p = _pick(seq, 512)
    tm_wo = _pick(seq, 256)
    xp, xs = x_prompt.reshape(m_p, d), x_sample.reshape(m_s, d)
    gla_p, conv_p, conv_s = [], [], []
    gla_s = jnp.zeros(state_gla.shape, F32)
    for l in range(depth):
        proj_p, fg_p = _win(xp, mod_p, l, norm1r, w_main, w_fg, seq, tm_p, BF16)
        proj_s, fg_s = _win(xs, mod_s, l, norm1r, w_main, w_fg, m_s, m_s, F32)
        oa_p, sg = _gla_chunk(proj_p, fg_p, l, w_fg2p, b_fg2r, gain_r, nb_p, seq, heads, dk, dv, _pick(seq, 256))
        gla_p.append(sg)
        oa_s, gla_s = _gla_step(proj_s, fg_s, state_gla, gla_s, l, w_fg2p, b_fg2r, gain_r, heads, dk, dv, rank,
                                  bt=_pick(m_s, 16))
        ob_p, sc = _conv_seq(proj_p, l, conv_w, nb_p, seq, off_conv, cw)
        conv_p.append(sc)
        ob_s, sc = _conv_step(proj_s, state_conv, l, conv_w, off_conv, cw)
        conv_s.append(sc)
        y_p = _merge(oa_p, ob_p, proj_p, l, w_pab, w_pbb, off_ga, tm_p)
        y_s = _merge(oa_s, ob_s, proj_s, l, w_pab, w_pbb, off_ga, m_s)
        if l % 2 == 0:
            x1_p, h_p = _wo(y_p, xp, mod_p, l, norm2r, w_ob, None, 0, 0, seq, tm_wo)
            x1_s, h_s = _wo(y_s, xs, mod_s, l, norm2r, w_ob, None, 0, 0, m_s, m_s)
            xp = _ffn(h_p, x1_p, mod_p, l, l // 2, d_w1, d_w3, d_w2, seq, tm_p)
            xs = _ffn(h_s, x1_s, mod_s, l, l // 2, d_w1, d_w3, d_w2, m_s, m_s)
        else:
            lm = l // 2
            x1_p, h_p, route_p = _wo(y_p, xp, mod_p, l, norm2r, w_ob, router_p, lm, n_experts, seq, tm_wo)
            x1_s, h_s, route_s = _wo(y_s, xs, mod_s, l, norm2r, w_ob, router_p, lm, n_experts, m_s, m_s)
            h_all = jnp.concatenate([h_p, h_s], axis=0)
            src, tile_expert, n_used, dest = _route_tables(jnp.concatenate([route_p, route_s], axis=0),
                                                           n_experts, MOE_TILE)
            ys = _ffn_grouped(h_all, src, tile_expert, n_used, lm, m_w1, m_w3, m_w2, MOE_TILE)
            xp = _combine(x1_p, mod_p, l, route_p, ys, dest, 0, m_p + m_s, seq, tm_wo)
            xs = _combine(x1_s, mod_s, l, route_s, ys, dest, m_p, m_p + m_s, m_s, m_s)
    y_p = _final_norm(xp, final_norm.reshape(1, d), tm_p).reshape(nb_p, seq, d)
    y_s = _final_norm(xs, final_norm.reshape(1, d), m_s).reshape(nb_s, 1, d)
    return (y_p, y_s, jnp.stack(gla_p), jnp.stack(conv_p), gla_s, jnp.stack(conv_s))
```

```python
import functools

import numpy as np
import jax
import jax.numpy as jnp
from jax import lax
from jax.experimental import pallas as pl
from jax.experimental.pallas import tpu as pltpu

F32 = jnp.float32
BF16 = jnp.bfloat16
I32 = jnp.int32
EPS = 1e-6
GATE_TEMP = 16.0
TOP_K = 2
LANES = 128
VMEM_LIMIT_BYTES = 56 * 1024 * 1024
MOE_TILE = 512


def _params(*sem):
    return pltpu.CompilerParams(dimension_semantics=sem, vmem_limit_bytes=VMEM_LIMIT_BYTES)


def _dot(a, b):
    return jnp.dot(a, b, preferred_element_type=F32)


def _dot_nt(a, b):
    return lax.dot_general(a, b, (((1,), (1,)), ((), ())), preferred_element_type=F32)


def _split(x):
    hi = x.astype(BF16)
    lo = (x - hi.astype(F32)).astype(BF16)
    return hi, lo


def _dot_f32(a, b):
    ah, al = _split(a)
    bh, bl = _split(b)
    return _dot(ah, bh) + (_dot(ah, bl) + _dot(al, bh))


def _silu(x):
    return x * jax.nn.sigmoid(x)


def _log_sigmoid(x):
    return jnp.minimum(x, 0.0) - jnp.log1p(jnp.exp(-jnp.abs(x)))


def _norm_mod(x, gain, scale, shift):
    y = x * lax.rsqrt(jnp.mean(x * x, axis=-1, keepdims=True) + EPS) * gain
    return y * (1.0 + scale) + shift


def _pick(n, pref):
    t = min(n, pref)
    while n % t:
        t //= 2
    return t


def _mod_spec(mod, l, k, grp, nargs):
    r, d = mod.shape[2], mod.shape[3] // 6
    if nargs == 1:
        return pl.BlockSpec((None, None, r, d), lambda i: (l, grp(i), 0, k))
    return pl.BlockSpec((None, None, r, d), lambda i, j: (l, grp(i), 0, k))


def _adaln_kernel(c_ref, w_ref, b_ref, o_ref):
    a = _silu(c_ref[...]).astype(BF16)
    o_ref[...] = _dot(a, w_ref[...].astype(BF16)) + b_ref[...]


def _adaln(c, w_ada, b_ada):
    depth, d, n = w_ada.shape
    r = c.shape[0]
    tn = _pick(n, 1024)
    return pl.pallas_call(
        _adaln_kernel,
        grid=(depth, n // tn),
        in_specs=[
            pl.BlockSpec((r, d), lambda l, j: (0, 0)),
            pl.BlockSpec((None, d, tn), lambda l, j: (l, 0, j)),
            pl.BlockSpec((None, 1, tn), lambda l, j: (l, 0, j)),
        ],
        out_specs=pl.BlockSpec((None, r, tn), lambda l, j: (l, 0, j)),
        out_shape=jax.ShapeDtypeStruct((depth, r, n), F32),
        compiler_params=_params("arbitrary", "arbitrary"),
        name="adaln",
    )(c, w_ada, b_ada.reshape(depth, 1, n))


def _stage_w_in_kernel(wt_ref, fgt_ref, main_ref, fg_ref, *, rank):
    main_ref[...] = jnp.transpose(wt_ref[0]).astype(BF16)

    @pl.when(pl.program_id(1) == 0)
    def _():
        fg = jnp.transpose(fgt_ref[...])
        lane = lax.broadcasted_iota(jnp.int32, fg.shape, 1)
        fg_ref[...] = jnp.where(lane < rank, fg, 0.0).astype(BF16)


def _stage_w_in(w_in, off, rank):
    depth, d, n = w_in.shape
    w_t = jnp.swapaxes(w_in, 1, 2)
    tc = _pick(int(np.gcd(off, n - rank - off)), 512)
    assert off % LANES == 0 and rank % 8 == 0
    src_row = lambda j: pl.multiple_of(j * tc + jnp.where(j * tc >= off, rank, 0), 8)
    return pl.pallas_call(
        functools.partial(_stage_w_in_kernel, rank=rank),
        grid=(depth, (n - rank) // tc),
        in_specs=[
            pl.BlockSpec((pl.Element(1), pl.Element(tc), pl.Element(d)), lambda l, j: (l, src_row(j), 0)),
            pl.BlockSpec((None, LANES, d), lambda l, j: (l, off // LANES, 0)),
        ],
        out_specs=[
            pl.BlockSpec((None, d, tc), lambda l, j: (l, 0, j)),
            pl.BlockSpec((None, d, LANES), lambda l, j: (l, 0, 0)),
        ],
        out_shape=[jax.ShapeDtypeStruct((depth, d, n - rank), BF16), jax.ShapeDtypeStruct((depth, d, LANES), BF16)],
        compiler_params=_params("arbitrary", "arbitrary"),
        name="stage_w_in",
    )(w_t, w_t)


def _win_kernel(x_ref, sh_ref, sc_ref, gain_ref, w_ref, wfg_ref, proj_ref, fg_ref, h_scr):
    @pl.when(pl.program_id(1) == 0)
    def _():
        h = _norm_mod(x_ref[...], gain_ref[...], sc_ref[...], sh_ref[...]).astype(BF16)
        h_scr[...] = h
        fg_ref[...] = _dot(h, wfg_ref[...])

    proj_ref[...] = _dot(h_scr[...], w_ref[...]).astype(proj_ref.dtype)


def _win(x, mod, l, gain, w_main, w_fg, rows_per_group, tm, out_dtype):
    m, d = x.shape
    n = w_main.shape[2]
    tn = _pick(n, 2048)
    grp = lambda i: (i * tm) // rows_per_group
    return pl.pallas_call(
        _win_kernel,
        grid=(m // tm, n // tn),
        in_specs=[
            pl.BlockSpec((tm, d), lambda i, j: (i, 0)),
            _mod_spec(mod, l, 0, grp, 2),
            _mod_spec(mod, l, 1, grp, 2),
            pl.BlockSpec((None, 1, d), lambda i, j: (l, 0, 0)),
            pl.BlockSpec((None, d, tn), lambda i, j: (l, 0, j)),
            pl.BlockSpec((None, d, LANES), lambda i, j: (l, 0, 0)),
        ],
        out_specs=[
            pl.BlockSpec((tm, tn), lambda i, j: (i, j)),
            pl.BlockSpec((tm, LANES), lambda i, j: (i, 0)),
        ],
        out_shape=[jax.ShapeDtypeStruct((m, n), out_dtype), jax.ShapeDtypeStruct((m, LANES), F32)],
        scratch_shapes=[pltpu.VMEM((tm, d), BF16)],
        compiler_params=_params("arbitrary", "arbitrary"),
        name="win",
    )(x, mod, mod, gain, w_main, w_fg)


GLA_VPU_LEVEL_MIN = 16


def _gla_tables(c):
    i = np.arange(c)[:, None]
    j = np.arange(c)[None, :]
    coefs = [(j <= i)]
    masks = [(i == j)]
    sizes = []
    b = c
    while b >= 2:
        mid = (i // b) * b + b // 2 - 1
        if b < GLA_VPU_LEVEL_MIN:
            coefs.append(((j > mid) & (j <= i)) | ((j > i) & (j <= mid)))
        masks.append((i // b == j // b) & (i % b >= b // 2) & (j % b < b // 2))
        sizes.append(b)
        b //= 2
    return np.concatenate(coefs, 0).astype(np.float32), np.stack(masks).astype(np.float32), tuple(sizes)


def _gla_chunk_kernel(q_ref, k_ref, v_ref, g_ref, fg_ref, wfg2_ref, bfg_ref, gain_ref, coef_ref, mask_ref,
                      o_ref, s_out_ref, s_scr, *, scale, chunk, sizes, heads):
    ci = pl.program_id(1)

    @pl.when(ci == 0)
    def _():
        s_scr[...] = jnp.zeros_like(s_scr)

    c = chunk
    dk, dv = s_scr.shape[1], s_scr.shape[2]
    log_a_all = _log_sigmoid(_dot_f32(fg_ref[...], wfg2_ref[...]) + bfg_ref[...]) * (1.0 / GATE_TEMP)
    coef = coef_ref[...]
    nt = min(c, LANES)
    for h in range(heads):
        log_a = log_a_all[:, h * dk:(h + 1) * dk]
        la_hi, la_mid = _split(log_a)
        la_lo = (log_a - la_hi.astype(F32) - la_mid.astype(F32)).astype(BF16)
        expo = _dot(coef, la_hi) + (_dot(coef, la_mid) + _dot(coef, la_lo))
        cum = expo[0:c]
        d_cum = jnp.exp(cum)
        d_tail = jnp.exp(cum[c - 1:c] - cum)

        q = q_ref[:, h * dk:(h + 1) * dk].astype(F32) * scale
        k = k_ref[:, h * dk:(h + 1) * dk].astype(F32)
        v = v_ref[:, h * dv:(h + 1) * dv].astype(BF16)

        scores = mask_ref[0] * _dot_nt(q.astype(BF16), k.astype(BF16))
        n_small = 0
        for lv, b in enumerate(sizes):
            if b >= GLA_VPU_LEVEL_MIN:
                blocks = cum.reshape(c // b, b, dk)
                d_lv = jnp.exp(-jnp.abs(blocks - blocks[:, b // 2 - 1:b // 2, :])).reshape(c, dk)
            else:
                n_small += 1
                d_lv = jnp.exp(expo[n_small * c:(n_small + 1) * c])
            scores = scores + mask_ref[1 + lv] * _dot_nt((q * d_lv).astype(BF16), (k * d_lv).astype(BF16))

        s = s_scr[h]
        o = _dot(scores.astype(BF16), v) + _dot((q * d_cum).astype(BF16), s.astype(BF16))

        a_col = jnp.transpose(d_cum[c - nt:c])[:, nt - 1:nt]
        k_tail_t = jnp.transpose(k * d_tail).astype(BF16)
        s_scr[h] = a_col * s + _dot(k_tail_t, v)

        o = o * lax.rsqrt(jnp.mean(o * o, axis=-1, keepdims=True) + EPS) * gain_ref[...]
        o_ref[:, h * dv:(h + 1) * dv] = (o * _silu(g_ref[:, h * dv:(h + 1) * dv].astype(F32))).astype(o_ref.dtype)

    @pl.when(ci == pl.num_programs(1) - 1)
    def _():
        s_out_ref[...] = s_scr[...]


def _gla_chunk(proj, fg, l, w_fg2, b_fg2, gain, batch, seq, heads, dk, dv, chunk):
    m = proj.shape[0]
    kw, vw = heads * dk, heads * dv
    nc = seq // chunk
    coef, mask, sizes = _gla_tables(chunk)
    rank_pad = w_fg2.shape[1]
    row = lambda b, c: b * nc + c
    kern = functools.partial(_gla_chunk_kernel, scale=float(dk) ** -0.5, chunk=chunk, sizes=sizes, heads=heads)
    return pl.pallas_call(
        kern,
        grid=(batch, nc),
        in_specs=[
            pl.BlockSpec((chunk, kw), lambda b, c: (row(b, c), 0)),
            pl.BlockSpec((chunk, kw), lambda b, c: (row(b, c), 1)),
            pl.BlockSpec((chunk, vw), lambda b, c: (row(b, c), (2 * kw) // vw)),
            pl.BlockSpec((chunk, vw), lambda b, c: (row(b, c), (2 * kw) // vw + 1)),
            pl.BlockSpec((chunk, LANES), lambda b, c: (row(b, c), 0)),
            pl.BlockSpec((None, rank_pad, kw), lambda b, c: (l, 0, 0)),
            pl.BlockSpec((None, 1, kw), lambda b, c: (l, 0, 0)),
            pl.BlockSpec((None, 1, dv), lambda b, c: (l, 0, 0)),
            pl.BlockSpec(coef.shape, lambda b, c: (0, 0)),
            pl.BlockSpec(mask.shape, lambda b, c: (0, 0, 0)),
        ],
        out_specs=[
            pl.BlockSpec((chunk, vw), lambda b, c: (row(b, c), 0)),
            pl.BlockSpec((None, heads, dk, dv), lambda b, c: (b, 0, 0, 0)),
        ],
        out_shape=[jax.ShapeDtypeStruct((m, vw), BF16), jax.ShapeDtypeStruct((batch, heads, dk, dv), F32)],
        scratch_shapes=[pltpu.VMEM((heads, dk, dv), F32)],
        compiler_params=_params("arbitrary", "arbitrary"),
        name="gla_chunk",
    )(proj, proj, proj, proj, fg, w_fg2, b_fg2, gain, jnp.asarray(coef, BF16), jnp.asarray(mask, F32))


def _gla_step_kernel(qt_ref, kt_ref, fgt_ref, wfg2t_ref, bfgt_ref, v_ref, g_ref, gain_ref, s_ref, carry_ref,
                     o_ref, s_out_ref, *, scale, bt, rank):
    del carry_ref
    w_t = wfg2t_ref[...]
    fg_t = fgt_ref[...]
    xg = bfgt_ref[...]
    for r in range(rank):
        xg = xg + w_t[:, r:r + 1] * fg_t[r:r + 1, :]
    a_t = jnp.exp(_log_sigmoid(xg) * (1.0 / GATE_TEMP))
    q_t = qt_ref[...] * scale
    k_t = kt_ref[...]
    for j in range(bt):
        s_new = a_t[:, j:j + 1] * s_ref[j] + k_t[:, j:j + 1] * v_ref[j:j + 1, :]
        s_out_ref[j] = s_new
        o = jnp.sum(q_t[:, j:j + 1] * s_new, axis=0, keepdims=True)
        o = o * lax.rsqrt(jnp.mean(o * o, axis=-1, keepdims=True) + EPS) * gain_ref[...]
        o_ref[j:j + 1, :] = o * _silu(g_ref[j:j + 1, :])


def _gla_step(proj, fg, state, new_state, l, w_fg2, b_fg2, gain, heads, dk, dv, rank, bt):
    bs = proj.shape[0]
    kw, vw = heads * dk, heads * dv
    nb = bs // bt
    rank_pad = w_fg2.shape[1]
    to_cols = lambda a: a.reshape(nb, bt, heads, dk).transpose(2, 0, 3, 1)
    q_t = to_cols(proj[:, :kw])
    k_t = to_cols(proj[:, kw:2 * kw])
    fg_t = fg.reshape(nb, bt, rank_pad).transpose(0, 2, 1)
    w_t = w_fg2[l].reshape(rank_pad, heads, dk).transpose(1, 2, 0)
    b_t = b_fg2[l].reshape(heads, dk, 1)
    kern = functools.partial(_gla_step_kernel, scale=float(dk) ** -0.5, bt=bt, rank=rank)
    state_spec = pl.BlockSpec((None, bt, None, dk, dv), lambda i, h: (l, i, h, 0, 0))
    in_specs = [
        pl.BlockSpec((None, None, dk, bt), lambda i, h: (h, i, 0, 0)),
        pl.BlockSpec((None, None, dk, bt), lambda i, h: (h, i, 0, 0)),
        pl.BlockSpec((None, rank_pad, bt), lambda i, h: (i, 0, 0)),
        pl.BlockSpec((None, dk, rank_pad), lambda i, h: (h, 0, 0)),
        pl.BlockSpec((None, dk, 1), lambda i, h: (h, 0, 0)),
        pl.BlockSpec((bt, dv), lambda i, h: (i, (2 * kw) // dv + h)),
        pl.BlockSpec((bt, dv), lambda i, h: (i, (2 * kw + vw) // dv + h)),
        pl.BlockSpec((None, 1, dv), lambda i, h: (l, 0, 0)),
        state_spec,
        pl.BlockSpec(memory_space=pl.ANY),
    ]
    args = [q_t, k_t, fg_t, w_t, b_t, proj, proj, gain, state, new_state]
    return pl.pallas_call(
        kern,
        grid=(nb, heads),
        in_specs=in_specs,
        out_specs=[pl.BlockSpec((bt, dv), lambda i, h: (i, h)), state_spec],
        out_shape=[jax.ShapeDtypeStruct((bs, vw), F32), jax.ShapeDtypeStruct(state.shape, F32)],
        input_output_aliases={len(args) - 1: 1},
        compiler_params=_params("arbitrary", "arbitrary"),
        name="gla_step",
    )(*args)


def _conv_seq_kernel(cb_ref, cc_ref, ch_ref, w_ref, o_ref, st_ref):
    u = cc_ref[...].astype(F32) * ch_ref[...].astype(F32)
    t = u.shape[0]
    rows = lax.broadcasted_iota(jnp.int32, u.shape, 0)
    u1 = jnp.where(rows >= 1, pltpu.roll(u, 1, 0), 0.0)
    u2 = jnp.where(rows >= 2, pltpu.roll(u, 2, 0), 0.0)
    w = w_ref[...]
    conv = w[0:1] * u2 + w[1:2] * u1 + w[2:3] * u
    o_ref[...] = (cb_ref[...].astype(F32) * conv).astype(o_ref.dtype)
    st_ref[...] = u[t - 2:t]


def _conv_seq(proj, l, conv_w, batch, seq, off, cw):
    assert conv_w.shape[1] == 3 and seq >= 2
    m = proj.shape[0]
    tw = _pick(cw, 512)
    nb = off // tw
    return pl.pallas_call(
        _conv_seq_kernel,
        grid=(batch, cw // tw),
        in_specs=[
            pl.BlockSpec((seq, tw), lambda b, j: (b, nb + j)),
            pl.BlockSpec((seq, tw), lambda b, j: (b, nb + cw // tw + j)),
            pl.BlockSpec((seq, tw), lambda b, j: (b, nb + 2 * (cw // tw) + j)),
            pl.BlockSpec((None, 3, tw), lambda b, j: (l, 0, j)),
        ],
        out_specs=[
            pl.BlockSpec((seq, tw), lambda b, j: (b, j)),
            pl.BlockSpec((None, 2, tw), lambda b, j: (b, 0, j)),
        ],
        out_shape=[jax.ShapeDtypeStruct((m, cw), BF16), jax.ShapeDtypeStruct((batch, 2, cw), F32)],
        compiler_params=_params("arbitrary", "arbitrary"),
        name="conv_seq",
    )(proj, proj, proj, conv_w)


def _conv_step_kernel(cb_ref, cc_ref, ch_ref, s0_ref, s1_ref, w_ref, o_ref, n0_ref, n1_ref):
    u = cc_ref[...] * ch_ref[...]
    w = w_ref[...]
    s1 = s1_ref[...]
    o_ref[...] = cb_ref[...] * (w[0:1] * s0_ref[...] + w[1:2] * s1 + w[2:3] * u)
    n0_ref[...] = s1
    n1_ref[...] = u


def _conv_step(proj, state, l, conv_w, off, cw):
    assert conv_w.shape[1] == 3 and state.shape[2] == 2
    bs = proj.shape[0]
    tw = _pick(cw, 512)
    nb, nw = off // tw, cw // tw
    st = state.reshape(state.shape[0], bs, 2 * cw)
    col = lambda k: pl.BlockSpec((bs, tw), lambda j: (0, k + j))
    stc = lambda k: pl.BlockSpec((None, bs, tw), lambda j: (l, 0, k + j))
    o, n0, n1 = pl.pallas_call(
        _conv_step_kernel,
        grid=(nw,),
        in_specs=[col(nb), col(nb + nw), col(nb + 2 * nw), stc(0), stc(nw),
                  pl.BlockSpec((None, 3, tw), lambda j: (l, 0, j))],
        out_specs=[col(0), col(0), col(0)],
        out_shape=[jax.ShapeDtypeStruct((bs, cw), F32)] * 3,
        compiler_params=_params("arbitrary"),
        name="conv_step",
    )(proj, proj, proj, st, st, conv_w)
    return o, jnp.stack([n0, n1], axis=1)


def _merge_kernel(oa_ref, ob_ref, ga_ref, gb_ref, wa_ref, wb_ref, y_ref):
    ya = _dot(oa_ref[...].astype(BF16), wa_ref[...])
    yb = _dot(ob_ref[...].astype(BF16), wb_ref[...])
    y = jax.nn.sigmoid(ga_ref[...].astype(F32)) * ya + jax.nn.sigmoid(gb_ref[...].astype(F32)) * yb
    y_ref[...] = y.astype(y_ref.dtype)


def _merge(oa, ob, proj, l, w_pa, w_pb, off_ga, tm):
    m, d = oa.shape[0], w_pa.shape[2]
    tn = _pick(d, 1024)
    na, nbk = off_ga // tn, (off_ga + d) // tn
    return pl.pallas_call(
        _merge_kernel,
        grid=(m // tm, d // tn),
        in_specs=[
            pl.BlockSpec((tm, oa.shape[1]), lambda i, j: (i, 0)),
            pl.BlockSpec((tm, ob.shape[1]), lambda i, j: (i, 0)),
            pl.BlockSpec((tm, tn), lambda i, j: (i, na + j)),
            pl.BlockSpec((tm, tn), lambda i, j: (i, nbk + j)),
            pl.BlockSpec((None, w_pa.shape[1], tn), lambda i, j: (l, 0, j)),
            pl.BlockSpec((None, w_pb.shape[1], tn), lambda i, j: (l, 0, j)),
        ],
        out_specs=pl.BlockSpec((tm, tn), lambda i, j: (i, j)),
        out_shape=jax.ShapeDtypeStruct((m, d), BF16),
        compiler_params=_params("arbitrary", "arbitrary"),
        name="merge",
    )(oa, ob, proj, proj, w_pa, w_pb)


def _top2_route(logits, n_experts):
    lane = lax.broadcasted_iota(jnp.int32, logits.shape, 1).astype(F32)
    lg = jnp.where(lane < n_experts, logits, -jnp.inf)
    m1 = jnp.max(lg, axis=1, keepdims=True)
    i1 = jnp.min(jnp.where(lg == m1, lane, float(LANES)), axis=1, keepdims=True)
    lg2 = jnp.where(lane == i1, -jnp.inf, lg)
    m2 = jnp.max(lg2, axis=1, keepdims=True)
    i2 = jnp.min(jnp.where(lg2 == m2, lane, float(LANES)), axis=1, keepdims=True)
    e2 = jnp.exp(m2 - m1)
    den = 1.0 + e2
    return jnp.where(lane == 0.0, i1, jnp.where(lane == 1.0, i2, jnp.where(lane == 2.0, 1.0 / den, e2 / den)))


def _wo_kernel(y_ref, x_ref, g1_ref, sh_ref, sc_ref, gain_ref, w_ref, *rest, n_experts):
    if n_experts:
        router_ref, x1_ref, h_ref, route_ref = rest
    else:
        x1_ref, h_ref = rest
    x1 = x_ref[...] + g1_ref[...] * _dot(y_ref[...], w_ref[...])
    x1_ref[...] = x1
    h = _norm_mod(x1, gain_ref[...], sc_ref[...], sh_ref[...])
    h_ref[...] = h.astype(h_ref.dtype)
    if n_experts:
        route_ref[...] = _top2_route(_dot_f32(h, router_ref[...]), n_experts)


def _wo(y, x, mod, l, gain, w_o, router, lm, n_experts, rows_per_group, tm):
    m, d = x.shape
    grp = lambda i: (i * tm) // rows_per_group
    in_specs = [
        pl.BlockSpec((tm, d), lambda i: (i, 0)),
        pl.BlockSpec((tm, d), lambda i: (i, 0)),
        _mod_spec(mod, l, 2, grp, 1), _mod_spec(mod, l, 3, grp, 1), _mod_spec(mod, l, 4, grp, 1),
        pl.BlockSpec((None, 1, d), lambda i: (l, 0, 0)),
        pl.BlockSpec((None, d, d), lambda i: (l, 0, 0)),
    ]
    args = [y, x, mod, mod, mod, gain, w_o]
    out_specs = [pl.BlockSpec((tm, d), lambda i: (i, 0)), pl.BlockSpec((tm, d), lambda i: (i, 0))]
    out_shape = [jax.ShapeDtypeStruct((m, d), F32), jax.ShapeDtypeStruct((m, d), F32 if n_experts else BF16)]
    if n_experts:
        in_specs.append(pl.BlockSpec((None, d, LANES), lambda i: (lm, 0, 0)))
        args.append(router)
        out_specs.append(pl.BlockSpec((tm, LANES), lambda i: (i, 0)))
        out_shape.append(jax.ShapeDtypeStruct((m, LANES), F32))
    return pl.pallas_call(
        functools.partial(_wo_kernel, n_experts=n_experts),
        grid=(m // tm,),
        in_specs=in_specs,
        out_specs=out_specs,
        out_shape=out_shape,
        compiler_params=_params("arbitrary"),
        name="wo",
    )(*args)


def _ffn_kernel(h_ref, x_ref, g2_ref, w1_ref, w3_ref, w2_ref, o_ref, acc):
    f = pl.program_id(1)

    @pl.when(f == 0)
    def _():
        acc[...] = jnp.zeros_like(acc)

    h = h_ref[...]
    hid = _silu(_dot(h, w1_ref[...])) * _dot(h, w3_ref[...])
    acc[...] += _dot(hid.astype(BF16), w2_ref[...])

    @pl.when(f == pl.num_programs(1) - 1)
    def _():
        o_ref[...] = x_ref[...] + g2_ref[...] * acc[...]


def _ffn(h, x, mod, l, ld, w1, w3, w2, rows_per_group, tm):
    m, d = x.shape
    ff = w1.shape[2]
    tf = _pick(ff, 512)
    grp = lambda i: (i * tm) // rows_per_group
    return pl.pallas_call(
        _ffn_kernel,
        grid=(m // tm, ff // tf),
        in_specs=[
            pl.BlockSpec((tm, d), lambda i, f: (i, 0)),
            pl.BlockSpec((tm, d), lambda i, f: (i, 0)),
            _mod_spec(mod, l, 5, grp, 2),
            pl.BlockSpec((None, d, tf), lambda i, f: (ld, 0, f)),
            pl.BlockSpec((None, d, tf), lambda i, f: (ld, 0, f)),
            pl.BlockSpec((None, tf, d), lambda i, f: (ld, f, 0)),
        ],
        out_specs=pl.BlockSpec((tm, d), lambda i, f: (i, 0)),
        out_shape=jax.ShapeDtypeStruct((m, d), F32),
        scratch_shapes=[pltpu.VMEM((tm, d), F32)],
        compiler_params=_params("arbitrary", "arbitrary"),
        name="ffn",
    )(h, x, mod, w1, w3, w2)


def _route_tables(route, n_experts, tm):
    m = route.shape[0]
    p_rows = -(-(TOP_K * m + n_experts * (tm - 1)) // tm) * tm
    e_flat = jnp.concatenate([route[:, 0], route[:, 1]]).astype(I32)
    onehot = (e_flat[:, None] == jnp.arange(n_experts, dtype=I32)[None, :]).astype(I32)
    rank = jnp.sum((jnp.cumsum(onehot, axis=0) - 1) * onehot, axis=1)
    counts = jnp.sum(onehot, axis=0)
    padded = ((counts + tm - 1) // tm) * tm
    ends = jnp.cumsum(padded)
    dest = (jnp.sum(onehot * (ends - padded)[None, :], axis=1) + rank).astype(I32)
    tile_start = jnp.arange(p_rows // tm, dtype=I32) * tm
    tile_expert = jnp.minimum(jnp.sum((tile_start[:, None] >= ends[None, :]).astype(I32), axis=1), n_experts - 1)
    n_used = (ends[-1] // tm).astype(I32).reshape(1)
    src = jnp.zeros((p_rows,), I32).at[dest].set(jnp.tile(jnp.arange(m, dtype=I32), TOP_K))
    return src, tile_expert.astype(I32), n_used, dest


def _row_copy(table_ref, row, dst_ref, r, sem):
    return pltpu.make_async_copy(table_ref.at[pl.ds(row, 1), :], dst_ref.at[pl.ds(r, 1), :], sem)


def _start_rows(idx_ref, base, n, table_ref, dst_ref, sem):
    def start(r, carry):
        _row_copy(table_ref, idx_ref[base + r], dst_ref, r, sem).start()
        return carry

    lax.fori_loop(0, n, start, 0, unroll=8)


def _wait_rows(n, table_ref, dst_ref, sem):
    def wait(r, carry):
        _row_copy(table_ref, 0, dst_ref, r, sem).wait()
        return carry

    lax.fori_loop(0, n, wait, 0, unroll=8)


ROW_GROUP = 8


def _gather_rows(idx_ref, base, n, table_ref, dst_ref, sem):
    assert n % ROW_GROUP == 0

    def start(g, carry):
        for j in range(ROW_GROUP):
            r = g * ROW_GROUP + j
            _row_copy(table_ref, idx_ref[base + r], dst_ref, r, sem).start(priority=j % 2)
        return carry

    lax.fori_loop(0, n // ROW_GROUP, start, 0)
    _wait_rows(n, table_ref, dst_ref, sem)


def _ffn_grouped_kernel(src_ref, te_ref, nu_ref, h_hbm, w1_ref, w3_ref, w2_ref, o_ref, xg, xb, acc, sems):
    t, f = pl.program_id(0), pl.program_id(1)
    tm = o_ref.shape[0]
    n_used = nu_ref[0]
    used = t < n_used
    last = f == pl.num_programs(1) - 1

    @pl.when(used & (f == 0))
    def _():
        slot = t % 2

        @pl.when(t == 0)
        def _():
            _start_rows(src_ref, 0, tm, h_hbm, xg.at[0], sems.at[0])

        _wait_rows(tm, h_hbm, xg.at[slot], sems.at[slot])

        @pl.when(t + 1 < n_used)
        def _():
            _start_rows(src_ref, (t + 1) * tm, tm, h_hbm, xg.at[1 - slot], sems.at[1 - slot])

        xb[...] = xg[slot].astype(BF16)
        acc[...] = jnp.zeros_like(acc)

    @pl.when(used)
    def _():
        x = xb[...]
        hid = _silu(_dot(x, w1_ref[...])) * _dot(x, w3_ref[...])
        acc[...] += _dot(hid.astype(BF16), w2_ref[...])

    @pl.when(used & last)
    def _():
        o_ref[...] = acc[...]

    @pl.when(jnp.logical_not(used) & last)
    def _():
        o_ref[...] = jnp.zeros_like(o_ref)


def _ffn_grouped(h, src, tile_expert, n_used, lm, w1, w3, w2, tm):
    d = h.shape[1]
    p_rows = src.shape[0]
    ff = w1.shape[3]
    tf = _pick(ff, 512)
    nf = ff // tf
    fidx = lambda t, f, nu: jnp.where(t < nu[0], f, nf - 1)
    return pl.pallas_call(
        _ffn_grouped_kernel,
        grid_spec=pltpu.PrefetchScalarGridSpec(
            num_scalar_prefetch=3,
            grid=(p_rows // tm, nf),
            in_specs=[
                pl.BlockSpec(memory_space=pl.ANY),
                pl.BlockSpec((None, None, d, tf), lambda t, f, src, te, nu: (lm, te[t], 0, fidx(t, f, nu))),
                pl.BlockSpec((None, None, d, tf), lambda t, f, src, te, nu: (lm, te[t], 0, fidx(t, f, nu))),
                pl.BlockSpec((None, None, tf, d), lambda t, f, src, te, nu: (lm, te[t], fidx(t, f, nu), 0)),
            ],
            out_specs=pl.BlockSpec((tm, d), lambda t, f, src, te, nu: (t, 0)),
            scratch_shapes=[pltpu.VMEM((2, tm, d), h.dtype), pltpu.VMEM((tm, d), BF16), pltpu.VMEM((tm, d), F32),
                            pltpu.SemaphoreType.DMA((2,))],
        ),
        out_shape=jax.ShapeDtypeStruct((p_rows, d), F32),
        compiler_params=_params("arbitrary", "arbitrary"),
        name="moe_ffn",
    )(src, tile_expert, n_used, h, w1, w3, w2)


def _combine_kernel(dest_ref, x_ref, g2_ref, route_ref, ys_hbm, o_ref, ybuf, sem, *, row0, m_all):
    tm = x_ref.shape[0]
    base = row0 + pl.program_id(0) * tm
    for slot in range(TOP_K):
        _gather_rows(dest_ref, slot * m_all + base, tm, ys_hbm, ybuf.at[slot], sem)
    route = route_ref[...]
    f = route[:, 2:3] * ybuf[0] + route[:, 3:4] * ybuf[1]
    o_ref[...] = x_ref[...] + g2_ref[...] * f


def _combine(x1, mod, l, route, ys, dest, row0, m_all, rows_per_group, tm):
    m, d = x1.shape
    grp = lambda i: (i * tm) // rows_per_group
    r = mod.shape[2]
    return pl.pallas_call(
        functools.partial(_combine_kernel, row0=row0, m_all=m_all),
        grid_spec=pltpu.PrefetchScalarGridSpec(
            num_scalar_prefetch=1,
            grid=(m // tm,),
            in_specs=[
                pl.BlockSpec((tm, d), lambda i, dest: (i, 0)),
                pl.BlockSpec((None, None, r, d), lambda i, dest: (l, grp(i), 0, 5)),
                pl.BlockSpec((tm, LANES), lambda i, dest: (i, 0)),
                pl.BlockSpec(memory_space=pl.ANY),
            ],
            out_specs=pl.BlockSpec((tm, d), lambda i, dest: (i, 0)),
            scratch_shapes=[pltpu.VMEM((TOP_K, tm, d), F32), pltpu.SemaphoreType.DMA(())],
        ),
        out_shape=jax.ShapeDtypeStruct((m, d), F32),
        compiler_params=_params("arbitrary"),
        name="moe_combine",
    )(dest, x1, mod, route, ys)


def _final_norm_kernel(x_ref, gain_ref, o_ref):
    x = x_ref[...]
    o_ref[...] = x * lax.rsqrt(jnp.mean(x * x, axis=-1, keepdims=True) + EPS) * gain_ref[...]


def _final_norm(x, gain, tm):
    m, d = x.shape
    return pl.pallas_call(
        _final_norm_kernel,
        grid=(m // tm,),
        in_specs=[pl.BlockSpec((tm, d), lambda i: (i, 0)), pl.BlockSpec((1, d), lambda i: (0, 0))],
        out_specs=pl.BlockSpec((tm, d), lambda i: (i, 0)),
        out_shape=jax.ShapeDtypeStruct((m, d), F32),
        compiler_params=_params("arbitrary"),
        name="final_norm",
    )(x, gain)


def kernel(x_prompt, x_sample, state_gla, state_conv, c_prompt, c_sample, w_ada, b_ada, norm1, norm2, w_in, w_fg2,
           b_fg2, gla_gain, conv_w, w_pa, w_pb, w_o, dense_w1, dense_w3, dense_w2, router, moe_w1, moe_w3, moe_w2,
           final_norm):
    depth, d = norm1.shape
    nb_p, seq, _ = x_prompt.shape
    nb_s = x_sample.shape[0]
    assert x_sample.shape[1] == 1
    _, _, heads, dk, dv = state_gla.shape
    kw, vw = heads * dk, heads * dv
    rank = w_fg2.shape[1]
    cw = conv_w.shape[-1]
    n_experts = router.shape[-1]
    assert rank <= LANES and n_experts <= LANES and TOP_K == 2
    m_p, m_s = nb_p * seq, nb_s
    off_conv = 2 * kw + 2 * vw
    off_ga = off_conv + 3 * cw

    w_main, w_fg = _stage_w_in(w_in, off_conv, rank)
    w_fg2p = jnp.pad(w_fg2, ((0, 0), (0, LANES - rank), (0, 0)))
    b_fg2r = b_fg2.reshape(depth, 1, kw)
    gain_r = gla_gain.reshape(depth, 1, dv)
    norm1r, norm2r = norm1.reshape(depth, 1, d), norm2.reshape(depth, 1, d)
    w_pab, w_pbb, w_ob = w_pa.astype(BF16), w_pb.astype(BF16), w_o.astype(BF16)
    d_w1, d_w3, d_w2 = dense_w1.astype(BF16), dense_w3.astype(BF16), dense_w2.astype(BF16)
    m_w1, m_w3, m_w2 = moe_w1.astype(BF16), moe_w3.astype(BF16), moe_w2.astype(BF16)
    router_p = jnp.pad(router, ((0, 0), (0, 0), (0, LANES - n_experts)))

    c_all = jnp.concatenate([c_prompt, c_sample], axis=0)
    mod = _adaln(jnp.pad(c_all, ((0, (-c_all.shape[0]) % 16), (0, 0))), w_ada, b_ada)
    mod_p = mod[:, :nb_p].reshape(depth, nb_p, 1, 6 * d)
    mod_s = mod[:, nb_p:nb_p + nb_s].reshape(depth, 1, nb_s, 6 * d)

    tm_p = _pick(seq, 512)
    tm_wo = _pick(seq, 256)
    xp, xs = x_prompt.reshape(m_p, d), x_sample.reshape(m_s, d)
    gla_p, conv_p, conv_s = [], [], []
    gla_s = jnp.zeros(state_gla.shape, F32)
    for l in range(depth):
        proj_p, fg_p = _win(xp, mod_p, l, norm1r, w_main, w_fg, seq, tm_p, BF16)
        proj_s, fg_s = _win(xs, mod_s, l, norm1r, w_main, w_fg, m_s, m_s, F32)
        oa_p, sg = _gla_chunk(proj_p, fg_p, l, w_fg2p, b_fg2r, gain_r, nb_p, seq, heads, dk, dv, _pick(seq, 256))
        gla_p.append(sg)
        oa_s, gla_s = _gla_step(proj_s, fg_s, state_gla, gla_s, l, w_fg2p, b_fg2r, gain_r, heads, dk, dv, rank,
                                  bt=_pick(m_s, 16))
        ob_p, sc = _conv_seq(proj_p, l, conv_w, nb_p, seq, off_conv, cw)
        conv_p.append(sc)
        ob_s, sc = _conv_step(proj_s, state_conv, l, conv_w, off_conv, cw)
        conv_s.append(sc)
        y_p = _merge(oa_p, ob_p, proj_p, l, w_pab, w_pbb, off_ga, tm_p)
        y_s = _merge(oa_s, ob_s, proj_s, l, w_pab, w_pbb, off_ga, m_s)
        if l % 2 == 0:
            x1_p, h_p = _wo(y_p, xp, mod_p, l, norm2r, w_ob, None, 0, 0, seq, tm_wo)
            x1_s, h_s = _wo(y_s, xs, mod_s, l, norm2r, w_ob, None, 0, 0, m_s, m_s)
            xp = _ffn(h_p, x1_p, mod_p, l, l // 2, d_w1, d_w3, d_w2, seq, tm_p)
            xs = _ffn(h_s, x1_s, mod_s, l, l // 2, d_w1, d_w3, d_w2, m_s, m_s)
        else:
            lm = l // 2
            x1_p, h_p, route_p = _wo(y_p, xp, mod_p, l, norm2r, w_ob, router_p, lm, n_experts, seq, tm_wo)
            x1_s, h_s, route_s = _wo(y_s, xs, mod_s, l, norm2r, w_ob, router_p, lm, n_experts, m_s, m_s)
            h_all = jnp.concatenate([h_p, h_s], axis=0)
            src, tile_expert, n_used, dest = _route_tables(jnp.concatenate([route_p, route_s], axis=0),
                                                           n_experts, MOE_TILE)
            ys = _ffn_grouped(h_all, src, tile_expert, n_used, lm, m_w1, m_w3, m_w2, MOE_TILE)
            xp = _combine(x1_p, mod_p, l, route_p, ys, dest, 0, m_p + m_s, seq, tm_wo)
            xs = _combine(x1_s, mod_s, l, route_s, ys, dest, m_p, m_p + m_s, m_s, m_s)
    y_p = _final_norm(xp, final_norm.reshape(1, d), tm_p).reshape(nb_p, seq, d)
    y_s = _final_norm(xs, final_norm.reshape(1, d), m_s).reshape(nb_s, 1, d)
    return (y_p, y_s, jnp.stack(gla_p), jnp.stack(conv_p), gla_s, jnp.stack(conv_s))
```
